```python
import math
import jax, jax.numpy as jnp
from jax import lax
import numpy as np

D_MODEL = 1024
BATCH = 4
SEQ = 8192
DEPTH = 1

HEAD_DIM = 64
MOBA_HEADS = D_MODEL // (2 * HEAD_DIM)
DIFF_HEADS = D_MODEL // (4 * HEAD_DIM)
MOBA_WIDTH = MOBA_HEADS * HEAD_DIM
DIFF_QK_WIDTH = DIFF_HEADS * 2 * HEAD_DIM
DIFF_V_DIM = 2 * HEAD_DIM
DIFF_WIDTH = DIFF_HEADS * DIFF_V_DIM
MIX_WIDTH = MOBA_WIDTH + DIFF_WIDTH
IN_PROJ_WIDTH = 3 * MOBA_WIDTH + 2 * DIFF_QK_WIDTH + DIFF_WIDTH
MOBA_BLOCK = 256
MOBA_TOPK = 3
MOBA_QCHUNK = 32
DENSE_QBLOCK = 128
FFN_HIDDEN = ((8 * D_MODEL + 3 * 256 - 1) // (3 * 256)) * 256
ROPE_THETA = 10000.0
RMS_EPS = 1e-6

kernel_name = "hymba_moba_diffattn_swiglu"


def rms_norm(x, gain):
    xf = x.astype(jnp.float32)
    xf = xf * lax.rsqrt(jnp.mean(xf * xf, axis=-1, keepdims=True) + RMS_EPS)
    return (xf * gain.astype(jnp.float32)).astype(x.dtype)


def rope_tables(seq, dim):
    inv = 1.0 / (ROPE_THETA ** (jnp.arange(0, dim, 2, dtype=jnp.float32) / dim))
    ang = jnp.arange(seq, dtype=jnp.float32)[:, None] * inv[None, :]
    ang = jnp.concatenate([ang, ang], axis=-1)
    return jnp.cos(ang), jnp.sin(ang)


def apply_rope(x, cos, sin):
    half = x.shape[-1] // 2
    rot = jnp.concatenate([-x[..., half:], x[..., :half]], axis=-1)
    return (x.astype(jnp.float32) * cos + rot.astype(jnp.float32) * sin).astype(x.dtype)


def moba_attention(q, k, v):
    B, H, S, D = q.shape
    s_pad = -(-S // MOBA_BLOCK) * MOBA_BLOCK
    pad = ((0, 0), (0, 0), (0, s_pad - S), (0, 0))
    q, k, v = jnp.pad(q, pad), jnp.pad(k, pad), jnp.pad(v, pad)
    nb = s_pad // MOBA_BLOCK
    top_k = min(MOBA_TOPK, nb)
    scale = D ** -0.5
    kb = k.reshape(B, H, nb, MOBA_BLOCK, D)
    vb = v.reshape(B, H, nb, MOBA_BLOCK, D)
    k_mean = jnp.mean(kb.astype(jnp.float32), axis=3).astype(k.dtype)
    b_ix = jnp.arange(B)[:, None, None, None]
    h_ix = jnp.arange(H)[None, :, None, None]
    C = MOBA_QCHUNK

    def chunk(c):
        start = c * C
        blk = start // MOBA_BLOCK
        qc = lax.dynamic_slice_in_dim(q, start, C, axis=2)
        gate = jnp.einsum('bhqd,bhnd->bhqn', qc, k_mean).astype(jnp.float32)
        gate = jnp.where(jnp.arange(nb) < blk, gate, -jnp.inf)
        _, sel = lax.top_k(gate, top_k)
        sel_valid = jnp.arange(top_k) < blk
        k_sel = kb[b_ix, h_ix, sel]
        v_sel = vb[b_ix, h_ix, sel]
        s_sel = jnp.einsum('bhqd,bhqrkd->bhqrk', qc, k_sel).astype(jnp.float32) * scale
        s_sel = jnp.where(sel_valid[:, None], s_sel, -jnp.inf)
        s_sel = s_sel.reshape(B, H, C, top_k * MOBA_BLOCK)
        k_own = lax.dynamic_index_in_dim(kb, blk, axis=2, keepdims=False)
        v_own = lax.dynamic_index_in_dim(vb, blk, axis=2, keepdims=False)
        s_own = jnp.einsum('bhqd,bhkd->bhqk', qc, k_own).astype(jnp.float32) * scale
        q_pos = start + jnp.arange(C)
        k_pos = blk * MOBA_BLOCK + jnp.arange(MOBA_BLOCK)
        s_own = jnp.where(k_pos[None, :] <= q_pos[:, None], s_own, -jnp.inf)
        p = jax.nn.softmax(jnp.concatenate([s_sel, s_own], axis=-1), axis=-1)
        p_sel = p[..., :top_k * MOBA_BLOCK].reshape(B, H, C, top_k, MOBA_BLOCK).astype(v.dtype)
        p_own = p[..., top_k * MOBA_BLOCK:].astype(v.dtype)
        return (jnp.einsum('bhqrk,bhqrkd->bhqd', p_sel, v_sel)
                + jnp.einsum('bhqk,bhkd->bhqd', p_own, v_own))

    outs = lax.map(chunk, jnp.arange(s_pad // C))
    out = outs.transpose(1, 2, 0, 3, 4).reshape(B, H, s_pad, D)
    return out[:, :, :S]


def diff_attention(q, k, v, lam):
    B, H, _, S, D = q.shape
    scale = D ** -0.5
    QB = DENSE_QBLOCK
    k_pos = jnp.arange(S)

    def qblock(i):
        start = i * QB
        qb = lax.dynamic_slice_in_dim(q, start, QB, axis=3)
        s = jnp.einsum('bhmqd,bhmkd->bhmqk', qb, k).astype(jnp.float32) * scale
        mask = k_pos[None, :] <= (start + jnp.arange(QB))[:, None]
        p = jax.nn.softmax(jnp.where(mask, s, -jnp.inf), axis=-1)
        a = p[:, :, 0] - lam * p[:, :, 1]
        return jnp.einsum('bhqk,bhkd->bhqd', a.astype(v.dtype), v)

    outs = lax.map(qblock, jnp.arange(S // QB))
    return outs.transpose(1, 2, 0, 3, 4).reshape(B, H, S, v.shape[-1])


def setup_inputs(seed: int = 0) -> dict:
    key = jax.random.key(seed)
    ks = jax.random.split(key, 17)
    f32 = jnp.float32

    def nrm(k, shape, scale):
        return jax.random.normal(k, shape, f32) * scale

    def gain(k, shape):
        return 1.0 + 0.02 * jax.random.normal(k, shape, f32)

    L = DEPTH
    return {
        "x": jax.random.normal(ks[0], (BATCH, SEQ, D_MODEL), f32),
        "attn_norm": gain(ks[1], (L, D_MODEL)),
        "w_in": nrm(ks[2], (L, D_MODEL, IN_PROJ_WIDTH), D_MODEL ** -0.5),
        "moba_q_norm": gain(ks[3], (L, HEAD_DIM)),
        "moba_k_norm": gain(ks[4], (L, HEAD_DIM)),
        "diff_q_norm": gain(ks[5], (L, HEAD_DIM)),
        "diff_k_norm": gain(ks[6], (L, HEAD_DIM)),
        "lambda_q1": nrm(ks[7], (L, HEAD_DIM), 0.1),
        "lambda_k1": nrm(ks[8], (L, HEAD_DIM), 0.1),
        "lambda_q2": nrm(ks[9], (L, HEAD_DIM), 0.1),
        "lambda_k2": nrm(ks[10], (L, HEAD_DIM), 0.1),
        "diff_subln": gain(ks[11], (L, DIFF_V_DIM)),
        "w_out": nrm(ks[12], (L, MIX_WIDTH, D_MODEL), MIX_WIDTH ** -0.5),
        "ffn_norm": gain(ks[13], (L, D_MODEL)),
        "w_gate": nrm(ks[14], (L, D_MODEL, FFN_HIDDEN), D_MODEL ** -0.5),
        "w_up": nrm(ks[15], (L, D_MODEL, FFN_HIDDEN), D_MODEL ** -0.5),
        "w_down": nrm(ks[16], (L, FFN_HIDDEN, D_MODEL), FFN_HIDDEN ** -0.5),
    }


def reference(x, attn_norm, w_in, moba_q_norm, moba_k_norm, diff_q_norm, diff_k_norm,
              lambda_q1, lambda_k1, lambda_q2, lambda_k2, diff_subln, w_out,
              ffn_norm, w_gate, w_up, w_down):
    B, S, _ = x.shape
    cos, sin = rope_tables(S, HEAD_DIM)
    splits = [MOBA_WIDTH, 2 * MOBA_WIDTH, 3 * MOBA_WIDTH,
              3 * MOBA_WIDTH + DIFF_QK_WIDTH, 3 * MOBA_WIDTH + 2 * DIFF_QK_WIDTH]
    for l in range(DEPTH):
        lambda_init = 0.8 - 0.6 * math.exp(-0.3 * l)
        h = rms_norm(x, attn_norm[l])
        proj = jnp.einsum('bsd,de->bse', h, w_in[l])
        mq, mk, mv, dq, dk, dv = jnp.split(proj, splits, axis=-1)

        mq = rms_norm(mq.reshape(B, S, MOBA_HEADS, HEAD_DIM), moba_q_norm[l]).transpose(0, 2, 1, 3)
        mk = rms_norm(mk.reshape(B, S, MOBA_HEADS, HEAD_DIM), moba_k_norm[l]).transpose(0, 2, 1, 3)
        mv = mv.reshape(B, S, MOBA_HEADS, HEAD_DIM).transpose(0, 2, 1, 3)
        mq, mk = apply_rope(mq, cos, sin), apply_rope(mk, cos, sin)
        moba_out = moba_attention(mq, mk, mv)
        moba_out = moba_out.transpose(0, 2, 1, 3).reshape(B, S, MOBA_WIDTH)

        dq = rms_norm(dq.reshape(B, S, DIFF_HEADS, 2, HEAD_DIM), diff_q_norm[l]).transpose(0, 2, 3, 1, 4)
        dk = rms_norm(dk.reshape(B, S, DIFF_HEADS, 2, HEAD_DIM), diff_k_norm[l]).transpose(0, 2, 3, 1, 4)
        dv = dv.reshape(B, S, DIFF_HEADS, DIFF_V_DIM).transpose(0, 2, 1, 3)
        dq, dk = apply_rope(dq, cos, sin), apply_rope(dk, cos, sin)
        lam = (jnp.exp(jnp.sum(lambda_q1[l].astype(jnp.float32) * lambda_k1[l].astype(jnp.float32)))
               - jnp.exp(jnp.sum(lambda_q2[l].astype(jnp.float32) * lambda_k2[l].astype(jnp.float32)))
               + lambda_init)
        diff_out = diff_attention(dq, dk, dv, lam)
        diff_out = rms_norm(diff_out, diff_subln[l]) * (1.0 - lambda_init)
        diff_out = diff_out.transpose(0, 2, 1, 3).reshape(B, S, DIFF_WIDTH)

        mixed = jnp.concatenate([moba_out, diff_out], axis=-1)
        x = x + jnp.einsum('bse,ed->bsd', mixed, w_out[l])

        h = rms_norm(x, ffn_norm[l])
        g = jnp.einsum('bsd,df->bsf', h, w_gate[l])
        u = jnp.einsum('bsd,df->bsf', h, w_up[l])
        x = x + jnp.einsum('bsf,fd->bsd', jax.nn.silu(g) * u, w_down[l])
    return x
```

```python
import functools
import math

import jax
import jax.numpy as jnp
from jax import lax
from jax.experimental import pallas as pl
from jax.experimental.pallas import tpu as pltpu

HEAD_DIM = 64
LANES = 128
MOBA_BLOCK = 256
MOBA_TOPK = 3
ROPE_THETA = 10000.0
RMS_EPS = 1e-6
ATTN_BLOCK = MOBA_BLOCK
VMEM_LIMIT_BYTES = 56 * 1024 * 1024

_NT = (((1,), (1,)), ((), ()))
_F32 = jnp.float32
_BF16 = jnp.bfloat16


def _const_spec(shape):
    zeros = (0,) * len(shape)
    return pl.BlockSpec(shape, lambda *_: zeros, pipeline_mode=pl.Buffered(1))


def _inproj_body(x_ref, gain_ref, w_ref, gmat_ref, hgain_ref, cos_ref, sin_ref,
                 qk_ref, vt_ref, kmean_ref, *, qk_cols, v_cols, width):
    x = x_ref[0]
    ms = jnp.mean(x * x, axis=-1, keepdims=True)
    h = (x * lax.rsqrt(ms + RMS_EPS) * gain_ref[...]).astype(_BF16)
    proj = jnp.dot(h, w_ref[...], preferred_element_type=_F32)

    tm = x.shape[0]
    cos = cos_ref[...]
    sin = sin_ref[...]
    lane = lax.broadcasted_iota(jnp.int32, (tm, LANES), 1)
    first_half = (lane % HEAD_DIM) < (HEAD_DIM // 2)

    for gi, c0 in enumerate(qk_cols):
        p = proj[:, c0:c0 + width]
        msq = jnp.dot((p * p).astype(_BF16), gmat_ref[...], preferred_element_type=_F32)
        y = p * lax.rsqrt(msq + RMS_EPS) * hgain_ref[gi:gi + 1, :]
        for cc in range(width // LANES):
            yc = y[:, cc * LANES:(cc + 1) * LANES]
            rot = jnp.where(first_half, pltpu.roll(yc, LANES - HEAD_DIM // 2, 1),
                            pltpu.roll(yc, HEAD_DIM // 2, 1))
            oc = yc * cos + rot * sin
            qk_ref[0, :, gi * width + cc * LANES:gi * width + (cc + 1) * LANES] = oc.astype(_BF16)
            if gi == 1:
                for rb in range(tm // MOBA_BLOCK):
                    kmean_ref[0, 0, rb:rb + 1, cc * LANES:(cc + 1) * LANES] = jnp.mean(
                        oc[rb * MOBA_BLOCK:(rb + 1) * MOBA_BLOCK], axis=0, keepdims=True)

    for vi, c0 in enumerate(v_cols):
        for rb in range(tm // ATTN_BLOCK):
            v = proj[rb * ATTN_BLOCK:(rb + 1) * ATTN_BLOCK, c0:c0 + width]
            vt_ref[0, rb, vi * width:(vi + 1) * width, :] = v.T.astype(_BF16)


def _inproj(x, gain, w, gmat, hgain, cos, sin, *, tm):
    B, S, D = x.shape
    n_out = w.shape[1]
    width = n_out // 6
    qk_cols = (0, width, 3 * width, 4 * width)
    v_cols = (2 * width, 5 * width)
    body = functools.partial(_inproj_body, qk_cols=qk_cols, v_cols=v_cols, width=width)
    return pl.pallas_call(
        body,
        grid=(B, S // tm),
        in_specs=[
            pl.BlockSpec((1, tm, D), lambda b, t: (b, t, 0)),
            _const_spec((1, D)),
            _const_spec((D, n_out)),
            _const_spec((width, width)),
            _const_spec((4, width)),
            pl.BlockSpec((tm, LANES), lambda b, t: (t, 0)),
            pl.BlockSpec((tm, LANES), lambda b, t: (t, 0)),
        ],
        out_specs=[
            pl.BlockSpec((1, tm, 4 * width), lambda b, t: (b, t, 0)),
            pl.BlockSpec((1, tm // ATTN_BLOCK, 2 * width, ATTN_BLOCK), lambda b, t: (b, t, 0, 0)),
            pl.BlockSpec((1, 1, tm // MOBA_BLOCK, width), lambda b, t: (b, t, 0, 0)),
        ],
        out_shape=[
            jax.ShapeDtypeStruct((B, S, 4 * width), _BF16),
            jax.ShapeDtypeStruct((B, S // ATTN_BLOCK, 2 * width, ATTN_BLOCK), _BF16),
            jax.ShapeDtypeStruct((B, S // tm, tm // MOBA_BLOCK, width), _F32),
        ],
        compiler_params=pltpu.CompilerParams(
            dimension_semantics=("parallel", "parallel"), vmem_limit_bytes=VMEM_LIMIT_BYTES),
        name="inproj_qknorm_rope",
    )(x, gain, w, gmat, hgain, cos, sin)


def _half_masked(q, half):
    lane = lax.broadcasted_iota(jnp.int32, q.shape, 1)
    return jnp.where((lane // HEAD_DIM) == half, q, jnp.zeros_like(q))


def _diag_step(kd, qh, vt):
    s = lax.dot_general(kd, qh, _NT, preferred_element_type=_F32)
    kpos = lax.broadcasted_iota(jnp.int32, s.shape, 0)
    qpos = lax.broadcasted_iota(jnp.int32, s.shape, 1)
    s = jnp.where(kpos <= qpos, s, -jnp.inf)
    m = jnp.max(s, axis=0, keepdims=True)
    p = jnp.exp(s - m)
    l = jnp.sum(p, axis=0, keepdims=True)
    acc = jnp.dot(vt, p.astype(_BF16), preferred_element_type=_F32)
    return m, l, acc


def _past_step(kb, qh, vt, m_old, l_old, acc_old, selrow=None):
    s = lax.dot_general(kb, qh, _NT, preferred_element_type=_F32)
    bmax = jnp.max(s, axis=0, keepdims=True)
    if selrow is None:
        m_new = jnp.maximum(m_old, bmax)
        m_used = m_new
    else:
        m_new = jnp.where(selrow, jnp.maximum(m_old, bmax), m_old)
        m_used = jnp.where(selrow, m_new, jnp.inf)
    alpha = jnp.exp(m_old - m_new)
    p = jnp.exp(s - m_used)
    l_new = alpha * l_old + jnp.sum(p, axis=0, keepdims=True)
    acc_new = alpha * acc_old + jnp.dot(vt, p.astype(_BF16), preferred_element_type=_F32)
    return m_new, l_new, acc_new


def _moba_body(q_ref, k_ref, vt_ref, kmean_ref, o_ref, sel_ref, m_ref, l_ref, acc_ref, *, nb):
    i = pl.program_id(2)
    q = q_ref[0]
    blk = lax.broadcasted_iota(jnp.int32, (nb, ATTN_BLOCK), 0).astype(_F32)
    valid = blk < i.astype(_F32)
    kmean = kmean_ref[0].astype(_BF16)
    kd = k_ref[0, pl.ds(pl.multiple_of(i * ATTN_BLOCK, ATTN_BLOCK), ATTN_BLOCK), :]
    qhs = []
    for h in range(2):
        qh = _half_masked(q, h)
        qhs.append(qh)
        g = lax.dot_general(kmean, qh, _NT, preferred_element_type=_F32)
        g = jnp.where(valid, g, -jnp.inf)
        sel = jnp.zeros(g.shape, _F32)
        for _ in range(MOBA_TOPK):
            mx = jnp.max(g, axis=0, keepdims=True)
            idx = jnp.min(jnp.where(g == mx, blk, float(nb)), axis=0, keepdims=True)
            pick = (blk == idx) & valid
            sel = jnp.where(pick, 1.0, sel)
            g = jnp.where(pick, -jnp.inf, g)
        sel_ref[h] = sel
        m, l, acc = _diag_step(kd, qh, vt_ref[0, i, h * HEAD_DIM:(h + 1) * HEAD_DIM, :])
        m_ref[h] = m
        l_ref[h] = l
        acc_ref[h] = acc

    def body(j, carry):
        kb = k_ref[0, pl.ds(pl.multiple_of(j * ATTN_BLOCK, ATTN_BLOCK), ATTN_BLOCK), :]
        for h in range(2):
            onehot = jnp.where(blk == j.astype(_F32), sel_ref[h], 0.0)
            selrow = jnp.max(onehot, axis=0, keepdims=True) > 0.0
            m, l, acc = _past_step(kb, qhs[h], vt_ref[0, j, h * HEAD_DIM:(h + 1) * HEAD_DIM, :],
                                   m_ref[h], l_ref[h], acc_ref[h], selrow)
            m_ref[h] = m
            l_ref[h] = l
            acc_ref[h] = acc
        return carry

    lax.fori_loop(0, i, body, 0)

    o_t = jnp.concatenate([acc_ref[0] / l_ref[0], acc_ref[1] / l_ref[1]], axis=0)
    o_ref[0] = o_t.T.astype(o_ref.dtype)


def _moba_attention(qk, vt, kmean, *, width):
    B, S, _ = qk.shape
    nb = S // ATTN_BLOCK
    pairs = width // LANES
    body = functools.partial(_moba_body, nb=nb)
    return pl.pallas_call(
        body,
        grid=(B, pairs, nb),
        in_specs=[
            pl.BlockSpec((1, ATTN_BLOCK, LANES), lambda b, p, i: (b, i, p)),
            pl.BlockSpec((1, S, LANES), lambda b, p, i: (b, 0, pairs + p)),
            pl.BlockSpec((1, nb, LANES, ATTN_BLOCK), lambda b, p, i: (b, 0, p, 0)),
            pl.BlockSpec((1, nb, LANES), lambda b, p, i: (b, 0, p)),
        ],
        out_specs=pl.BlockSpec((1, ATTN_BLOCK, LANES), lambda b, p, i: (b, i, p)),
        out_shape=jax.ShapeDtypeStruct((B, S, width), _BF16),
        scratch_shapes=[
            pltpu.VMEM((2, nb, ATTN_BLOCK), _F32),
            pltpu.VMEM((2, 1, ATTN_BLOCK), _F32),
            pltpu.VMEM((2, 1, ATTN_BLOCK), _F32),
            pltpu.VMEM((2, HEAD_DIM, ATTN_BLOCK), _F32),
        ],
        compiler_params=pltpu.CompilerParams(
            dimension_semantics=("parallel", "parallel", "arbitrary"),
            vmem_limit_bytes=VMEM_LIMIT_BYTES),
        name="moba_attention",
    )(qk, qk, vt, kmean)


def _diff_body(q_ref, k_ref, vt_ref, lq1_ref, lk1_ref, lq2_ref, lk2_ref, subln_ref, o_ref,
               m_ref, l_ref, acc_ref, *, lambda_init):
    i = pl.program_id(2)
    q = q_ref[0]
    kd = k_ref[0, pl.ds(pl.multiple_of(i * ATTN_BLOCK, ATTN_BLOCK), ATTN_BLOCK), :]
    qhs = []
    for c in range(2):
        qh = _half_masked(q, c)
        qhs.append(qh)
        m, l, acc = _diag_step(kd, qh, vt_ref[0, i])
        m_ref[c] = m
        l_ref[c] = l
        acc_ref[c] = acc

    def body(j, carry):
        kb = k_ref[0, pl.ds(pl.multiple_of(j * ATTN_BLOCK, ATTN_BLOCK), ATTN_BLOCK), :]
        vt = vt_ref[0, j]
        for c in range(2):
            m, l, acc = _past_step(kb, qhs[c], vt, m_ref[c], l_ref[c], acc_ref[c])
            m_ref[c] = m
            l_ref[c] = l
            acc_ref[c] = acc
        return carry

    lax.fori_loop(0, i, body, 0)

    lam = (jnp.exp(jnp.sum(lq1_ref[...] * lk1_ref[...], axis=-1, keepdims=True))
           - jnp.exp(jnp.sum(lq2_ref[...] * lk2_ref[...], axis=-1, keepdims=True))
           + lambda_init)
    a_t = acc_ref[0] / l_ref[0] - lam * (acc_ref[1] / l_ref[1])
    a = a_t.T
    ms = jnp.mean(a * a, axis=-1, keepdims=True)
    out = (a * lax.rsqrt(ms + RMS_EPS) * subln_ref[...]) * (1.0 - lambda_init)
    o_ref[0] = out.astype(o_ref.dtype)


def _diff_attention(qk, vt, lq1, lk1, lq2, lk2, subln, *, width, lambda_init):
    B, S, _ = qk.shape
    nb = S // ATTN_BLOCK
    heads = width // LANES
    body = functools.partial(_diff_body, lambda_init=lambda_init)
    vec = lambda n: _const_spec((1, n))
    return pl.pallas_call(
        body,
        grid=(B, heads, nb),
        in_specs=[
            pl.BlockSpec((1, ATTN_BLOCK, LANES), lambda b, h, i: (b, i, 2 * heads + h)),
            pl.BlockSpec((1, S, LANES), lambda b, h, i: (b, 0, 3 * heads + h)),
            pl.BlockSpec((1, nb, LANES, ATTN_BLOCK), lambda b, h, i: (b, 0, heads + h, 0)),
            vec(HEAD_DIM), vec(HEAD_DIM), vec(HEAD_DIM), vec(HEAD_DIM), vec(LANES),
        ],
        out_specs=pl.BlockSpec((1, ATTN_BLOCK, LANES), lambda b, h, i: (b, i, h)),
        out_shape=jax.ShapeDtypeStruct((B, S, width), _BF16),
        scratch_shapes=[
            pltpu.VMEM((2, 1, ATTN_BLOCK), _F32),
            pltpu.VMEM((2, 1, ATTN_BLOCK), _F32),
            pltpu.VMEM((2, LANES, ATTN_BLOCK), _F32),
        ],
        compiler_params=pltpu.CompilerParams(
            dimension_semantics=("parallel", "parallel", "arbitrary"),
            vmem_limit_bytes=VMEM_LIMIT_BYTES),
        name="diff_attention",
    )(qk, qk, vt, lq1, lk1, lq2, lk2, subln)


def _out_ffn_body(x_ref, mo_ref, do_ref, wo_ref, gain_ref, wg_ref, wu_ref, wd_ref, y_ref, *, width):
    x = x_ref[...]
    att = (jnp.dot(mo_ref[...], wo_ref[0:width, :], preferred_element_type=_F32)
           + jnp.dot(do_ref[...], wo_ref[width:2 * width, :], preferred_element_type=_F32))
    x1 = x + att
    ms = jnp.mean(x1 * x1, axis=-1, keepdims=True)
    h = (x1 * lax.rsqrt(ms + RMS_EPS) * gain_ref[...]).astype(_BF16)
    g = jnp.dot(h, wg_ref[...], preferred_element_type=_F32)
    u = jnp.dot(h, wu_ref[...], preferred_element_type=_F32)
    a = (g * jax.nn.sigmoid(g) * u).astype(_BF16)
    y_ref[...] = x1 + jnp.dot(a, wd_ref[...], preferred_element_type=_F32)


def _out_ffn(x2, mo2, do2, wo, gain, wg, wu, wd, *, tm):
    T, D = x2.shape
    width = mo2.shape[1]
    F = wg.shape[1]
    body = functools.partial(_out_ffn_body, width=width)
    row = lambda n: pl.BlockSpec((tm, n), lambda t: (t, 0))
    return pl.pallas_call(
        body,
        grid=(T // tm,),
        in_specs=[row(D), row(width), row(width), _const_spec((2 * width, D)), _const_spec((1, D)),
                  _const_spec((D, F)), _const_spec((D, F)), _const_spec((F, D))],
        out_specs=row(D),
        out_shape=jax.ShapeDtypeStruct((T, D), _F32),
        compiler_params=pltpu.CompilerParams(
            dimension_semantics=("parallel",), vmem_limit_bytes=VMEM_LIMIT_BYTES),
        name="outproj_swiglu",
    )(x2, mo2, do2, wo, gain, wg, wu, wd)


def _rope_tables(seq):
    half = HEAD_DIM // 2
    inv = 1.0 / (ROPE_THETA ** (jnp.arange(0, HEAD_DIM, 2, dtype=_F32) / HEAD_DIM))
    ang = jnp.arange(seq, dtype=_F32)[:, None] * inv[None, :]
    cos = jnp.tile(jnp.cos(ang), (1, 2 * LANES // HEAD_DIM))
    sin = jnp.sin(ang)
    sin = jnp.tile(jnp.concatenate([-sin, sin], axis=-1), (1, LANES // HEAD_DIM))
    del half
    return cos, sin


def kernel(x, attn_norm, w_in, moba_q_norm, moba_k_norm, diff_q_norm, diff_k_norm,
           lambda_q1, lambda_k1, lambda_q2, lambda_k2, diff_subln, w_out,
           ffn_norm, w_gate, w_up, w_down):
    B, S, D = x.shape
    depth = w_in.shape[0]
    width = w_in.shape[2] // 6
    heads_per_group = width // HEAD_DIM
    tm_in = 512 if S % 512 == 0 else ATTN_BLOCK
    tm_ffn = 512 if (B * S) % 512 == 0 else ATTN_BLOCK
    cos, sin = _rope_tables(S)
    gmat = jnp.kron(jnp.eye(heads_per_group, dtype=_F32),
                    jnp.full((HEAD_DIM, HEAD_DIM), 1.0 / HEAD_DIM, _F32)).astype(_BF16)
    scale = HEAD_DIM ** -0.5
    for l in range(depth):
        lambda_init = 0.8 - 0.6 * math.exp(-0.3 * l)
        tile = lambda g: jnp.tile(g.astype(_F32), heads_per_group)
        hgain = jnp.stack([tile(moba_q_norm[l]) * scale, tile(moba_k_norm[l]),
                           tile(diff_q_norm[l]) * scale, tile(diff_k_norm[l])])
        qk, vt, kmean = _inproj(x, attn_norm[l][None, :], w_in[l].astype(_BF16), gmat, hgain,
                                cos, sin, tm=tm_in)
        kmean = kmean.reshape(B, S // MOBA_BLOCK, width)
        moba_out = _moba_attention(qk, vt, kmean, width=width)
        diff_out = _diff_attention(qk, vt, lambda_q1[l][None, :], lambda_k1[l][None, :],
                                   lambda_q2[l][None, :], lambda_k2[l][None, :],
                                   diff_subln[l][None, :], width=width, lambda_init=lambda_init)
        y = _out_ffn(x.reshape(B * S, D), moba_out.reshape(B * S, width),
                     diff_out.reshape(B * S, width), w_out[l].astype(_BF16), ffn_norm[l][None, :],
                     w_gate[l].astype(_BF16), w_up[l].astype(_BF16), w_down[l].astype(_BF16),
                     tm=tm_ffn)
        x = y.reshape(B, S, D)
    return x
```

```python
import functools
import math

import jax
import jax.numpy as jnp
from jax import lax
from jax.experimental import pallas as pl
from jax.experimental.pallas import tpu as pltpu

HEAD_DIM = 64
LANES = 128
MOBA_BLOCK = 256
MOBA_TOPK = 3
ROPE_THETA = 10000.0
RMS_EPS = 1e-6
ATTN_BLOCK = MOBA_BLOCK
ATTN_GROUPS = 2
GROUP_COLS = 2 * ATTN_BLOCK
NEG_INIT = -1e30
VMEM_LIMIT_BYTES = 56 * 1024 * 1024

_NT = (((1,), (1,)), ((), ()))
_F32 = jnp.float32
_BF16 = jnp.bfloat16


def _const_spec(shape):
    zeros = (0,) * len(shape)
    return pl.BlockSpec(shape, lambda *_: zeros, pipeline_mode=pl.Buffered(1))


def _inproj_body(x_ref, gain_ref, w_ref, gmat_ref, hgain_ref, cos_ref, sin_ref,
                 qk_ref, vt_ref, kmean_ref, *, qk_cols, v_cols, width):
    x = x_ref[0]
    ms = jnp.mean(x * x, axis=-1, keepdims=True)
    h = (x * lax.rsqrt(ms + RMS_EPS) * gain_ref[...]).astype(_BF16)
    proj = jnp.dot(h, w_ref[...], preferred_element_type=_F32)

    tm = x.shape[0]
    cos = cos_ref[...]
    sin = sin_ref[...]
    lane = lax.broadcasted_iota(jnp.int32, (tm, LANES), 1)
    first_half = (lane % HEAD_DIM) < (HEAD_DIM // 2)

    for gi, c0 in enumerate(qk_cols):
        p = proj[:, c0:c0 + width]
        msq = jnp.dot((p * p).astype(_BF16), gmat_ref[...], preferred_element_type=_F32)
        y = p * lax.rsqrt(msq + RMS_EPS) * hgain_ref[gi:gi + 1, :]
        for cc in range(width // LANES):
            yc = y[:, cc * LANES:(cc + 1) * LANES]
            rot = jnp.where(first_half, pltpu.roll(yc, LANES - HEAD_DIM // 2, 1),
                            pltpu.roll(yc, HEAD_DIM // 2, 1))
            oc = yc * cos + rot * sin
            qk_ref[0, :, gi * width + cc * LANES:gi * width + (cc + 1) * LANES] = oc.astype(_BF16)
            if gi == 1:
                for rb in range(tm // MOBA_BLOCK):
                    kmean_ref[0, 0, rb:rb + 1, cc * LANES:(cc + 1) * LANES] = jnp.mean(
                        oc[rb * MOBA_BLOCK:(rb + 1) * MOBA_BLOCK], axis=0, keepdims=True)

    for vi, c0 in enumerate(v_cols):
        for rb in range(tm // ATTN_BLOCK):
            v = proj[rb * ATTN_BLOCK:(rb + 1) * ATTN_BLOCK, c0:c0 + width]
            vt_ref[0, rb, vi * width:(vi + 1) * width, :] = v.T.astype(_BF16)


def _inproj(x, gain, w, gmat, hgain, cos, sin, *, tm):
    B, S, D = x.shape
    n_out = w.shape[1]
    width = n_out // 6
    qk_cols = (0, width, 3 * width, 4 * width)
    v_cols = (2 * width, 5 * width)
    body = functools.partial(_inproj_body, qk_cols=qk_cols, v_cols=v_cols, width=width)
    return pl.pallas_call(
        body,
        grid=(B, S // tm),
        in_specs=[
            pl.BlockSpec((1, tm, D), lambda b, t: (b, t, 0)),
            _const_spec((1, D)),
            _const_spec((D, n_out)),
            _const_spec((width, width)),
            _const_spec((4, width)),
            pl.BlockSpec((tm, LANES), lambda b, t: (t, 0)),
            pl.BlockSpec((tm, LANES), lambda b, t: (t, 0)),
        ],
        out_specs=[
            pl.BlockSpec((1, tm, 4 * width), lambda b, t: (b, t, 0)),
            pl.BlockSpec((1, tm // ATTN_BLOCK, 2 * width, ATTN_BLOCK), lambda b, t: (b, t, 0, 0)),
            pl.BlockSpec((1, 1, tm // MOBA_BLOCK, width), lambda b, t: (b, t, 0, 0)),
        ],
        out_shape=[
            jax.ShapeDtypeStruct((B, S, 4 * width), _BF16),
            jax.ShapeDtypeStruct((B, S // ATTN_BLOCK, 2 * width, ATTN_BLOCK), _BF16),
            jax.ShapeDtypeStruct((B, S // tm, tm // MOBA_BLOCK, width), _F32),
        ],
        compiler_params=pltpu.CompilerParams(
            dimension_semantics=("parallel", "parallel"), vmem_limit_bytes=VMEM_LIMIT_BYTES),
        name="inproj_qknorm_rope",
    )(x, gain, w, gmat, hgain, cos, sin)


def _stacked_queries(q_ref, g):
    q = q_ref[0, :, g * LANES:(g + 1) * LANES]
    lane = lax.broadcasted_iota(jnp.int32, q.shape, 1)
    zero = jnp.zeros_like(q)
    return jnp.concatenate([jnp.where(lane < HEAD_DIM, q, zero),
                            jnp.where(lane >= HEAD_DIM, q, zero)], axis=0)


def _flash_blocks(i, qcats, k_ref, vt_ref, s_ref, m_ref, l_ref, acc_ref, selrow_fn=None):
    groups = len(qcats)

    def scores(j, slot):
        kb = k_ref[0, pl.ds(pl.multiple_of(j * ATTN_BLOCK, ATTN_BLOCK), ATTN_BLOCK), :]
        for g in range(groups):
            s_ref[slot, :, g * GROUP_COLS:(g + 1) * GROUP_COLS] = lax.dot_general(
                kb[:, g * LANES:(g + 1) * LANES], qcats[g], _NT, preferred_element_type=_F32)

    def update(j, slot, diag):
        s = s_ref[slot]
        if diag:
            kpos = lax.broadcasted_iota(jnp.int32, s.shape, 0)
            qpos = lax.broadcasted_iota(jnp.int32, s.shape, 1) % ATTN_BLOCK
            s = jnp.where(kpos <= qpos, s, -jnp.inf)
        bmax = jnp.max(s, axis=0, keepdims=True)
        m_old = m_ref[...]
        if diag or selrow_fn is None:
            m_new = jnp.maximum(m_old, bmax)
            m_used = m_new
        else:
            selrow = selrow_fn(j)
            m_new = jnp.where(selrow, jnp.maximum(m_old, bmax), m_old)
            m_used = jnp.where(selrow, m_new, jnp.inf)
        alpha = jnp.exp2(m_old - m_new)
        p = jnp.exp2(s - m_used)
        l_ref[...] = alpha * l_ref[...] + jnp.sum(p, axis=0, keepdims=True)
        m_ref[...] = m_new
        pb = p.astype(_BF16)
        for g in range(groups):
            cols = slice(g * GROUP_COLS, (g + 1) * GROUP_COLS)
            acc_ref[g] = alpha[:, cols] * acc_ref[g] + jnp.dot(
                vt_ref[0, j, g * LANES:(g + 1) * LANES, :], pb[:, cols], preferred_element_type=_F32)

    m_ref[...] = jnp.full(m_ref.shape, NEG_INIT, _F32)
    l_ref[...] = jnp.zeros(l_ref.shape, _F32)
    acc_ref[...] = jnp.zeros(acc_ref.shape, _F32)
    scores(0, 0)

    def block_pair(t, carry):
        j = 2 * t
        scores(j + 1, 1)
        update(j, 0, False)
        scores(j + 2, 0)
        update(j + 1, 1, False)
        return carry

    lax.fori_loop(0, lax.shift_right_logical(i, 1), block_pair, 0)
    odd = (i & 1) == 1

    @pl.when(odd)
    def _():
        scores(i, 1)
        update(i - 1, 0, False)
        update(i, 1, True)

    @pl.when(jnp.logical_not(odd))
    def _():
        update(i, 0, True)


def _attn_scratch(nb_sel, groups):
    cols = groups * GROUP_COLS
    shapes = [pltpu.VMEM((2, ATTN_BLOCK, cols), _F32)]
    if nb_sel:
        shapes.append(pltpu.VMEM((nb_sel, cols), _F32))
    shapes += [pltpu.VMEM((1, cols), _F32), pltpu.VMEM((1, cols), _F32),
               pltpu.VMEM((groups, LANES, GROUP_COLS), _F32)]
    return shapes


def _moba_body(q_ref, k_ref, vt_ref, kmean_ref, o_ref, s_ref, sel_ref, m_ref, l_ref, acc_ref,
               *, nb, groups):
    i = pl.program_id(2)
    blk = lax.broadcasted_iota(jnp.int32, (nb, GROUP_COLS), 0).astype(_F32)
    valid = blk < i.astype(_F32)
    qcats = []
    for g in range(groups):
        qcat = _stacked_queries(q_ref, g)
        qcats.append(qcat)
        kmean = kmean_ref[0, :, g * LANES:(g + 1) * LANES].astype(_BF16)
        gate = lax.dot_general(kmean, qcat, _NT, preferred_element_type=_F32)
        gate = jnp.where(valid, gate, -jnp.inf)
        sel = jnp.zeros(gate.shape, _F32)
        for _ in range(MOBA_TOPK):
            mx = jnp.max(gate, axis=0, keepdims=True)
            idx = jnp.min(jnp.where(gate == mx, blk, float(nb)), axis=0, keepdims=True)
            pick = (blk == idx) & valid
            sel = jnp.where(pick, 1.0, sel)
            gate = jnp.where(pick, -jnp.inf, gate)
        sel_ref[:, g * GROUP_COLS:(g + 1) * GROUP_COLS] = sel

    blk_all = lax.broadcasted_iota(jnp.int32, sel_ref.shape, 0)

    def selrow_fn(j):
        return jnp.max(jnp.where(blk_all == j, sel_ref[...], 0.0), axis=0, keepdims=True) > 0.0

    _flash_blocks(i, qcats, k_ref, vt_ref, s_ref, m_ref, l_ref, acc_ref, selrow_fn)

    for g in range(groups):
        acc = acc_ref[g]
        l = l_ref[:, g * GROUP_COLS:(g + 1) * GROUP_COLS]
        o_t = jnp.concatenate(
            [acc[:HEAD_DIM, :ATTN_BLOCK] / l[:, :ATTN_BLOCK],
             acc[HEAD_DIM:, ATTN_BLOCK:] / l[:, ATTN_BLOCK:]], axis=0)
        o_ref[0, :, g * LANES:(g + 1) * LANES] = o_t.T.astype(o_ref.dtype)


def _moba_attention(qk, vt, kmean, *, width, groups):
    B, S, _ = qk.shape
    nb = S // ATTN_BLOCK
    gw = groups * LANES
    chunks = width // gw
    body = functools.partial(_moba_body, nb=nb, groups=groups)
    return pl.pallas_call(
        body,
        grid=(B, chunks, nb),
        in_specs=[
            pl.BlockSpec((1, ATTN_BLOCK, gw), lambda b, p, i: (b, i, p)),
            pl.BlockSpec((1, S, gw), lambda b, p, i: (b, 0, chunks + p)),
            pl.BlockSpec((1, nb, gw, ATTN_BLOCK), lambda b, p, i: (b, 0, p, 0)),
            pl.BlockSpec((1, nb, gw), lambda b, p, i: (b, 0, p)),
        ],
        out_specs=pl.BlockSpec((1, ATTN_BLOCK, gw), lambda b, p, i: (b, i, p)),
        out_shape=jax.ShapeDtypeStruct((B, S, width), _BF16),
        scratch_shapes=_attn_scratch(nb, groups),
        compiler_params=pltpu.CompilerParams(
            dimension_semantics=("parallel", "parallel", "arbitrary"),
            vmem_limit_bytes=VMEM_LIMIT_BYTES),
        name="moba_attention",
    )(qk, qk, vt, kmean)


def _diff_body(q_ref, k_ref, vt_ref, lq1_ref, lk1_ref, lq2_ref, lk2_ref, subln_ref, o_ref,
               s_ref, m_ref, l_ref, acc_ref, *, lambda_init, groups):
    i = pl.program_id(2)
    qcats = [_stacked_queries(q_ref, g) for g in range(groups)]
    _flash_blocks(i, qcats, k_ref, vt_ref, s_ref, m_ref, l_ref, acc_ref)

    lam = (jnp.exp(jnp.sum(lq1_ref[...] * lk1_ref[...], axis=-1, keepdims=True))
           - jnp.exp(jnp.sum(lq2_ref[...] * lk2_ref[...], axis=-1, keepdims=True))
           + lambda_init)
    for g in range(groups):
        acc = acc_ref[g]
        l = l_ref[:, g * GROUP_COLS:(g + 1) * GROUP_COLS]
        a_t = (acc[:, :ATTN_BLOCK] / l[:, :ATTN_BLOCK]
               - lam * (acc[:, ATTN_BLOCK:] / l[:, ATTN_BLOCK:]))
        a = a_t.T
        ms = jnp.mean(a * a, axis=-1, keepdims=True)
        out = (a * lax.rsqrt(ms + RMS_EPS) * subln_ref[...]) * (1.0 - lambda_init)
        o_ref[0, :, g * LANES:(g + 1) * LANES] = out.astype(o_ref.dtype)


def _diff_attention(qk, vt, lq1, lk1, lq2, lk2, subln, *, width, lambda_init, groups):
    B, S, _ = qk.shape
    nb = S // ATTN_BLOCK
    gw = groups * LANES
    chunks = width // gw
    body = functools.partial(_diff_body, lambda_init=lambda_init, groups=groups)
    vec = lambda n: _const_spec((1, n))
    return pl.pallas_call(
        body,
        grid=(B, chunks, nb),
        in_specs=[
            pl.BlockSpec((1, ATTN_BLOCK, gw), lambda b, h, i: (b, i, 2 * chunks + h)),
            pl.BlockSpec((1, S, gw), lambda b, h, i: (b, 0, 3 * chunks + h)),
            pl.BlockSpec((1, nb, gw, ATTN_BLOCK), lambda b, h, i: (b, 0, chunks + h, 0)),
            vec(HEAD_DIM), vec(HEAD_DIM), vec(HEAD_DIM), vec(HEAD_DIM), vec(LANES),
        ],
        out_specs=pl.BlockSpec((1, ATTN_BLOCK, gw), lambda b, h, i: (b, i, h)),
        out_shape=jax.ShapeDtypeStruct((B, S, width), _BF16),
        scratch_shapes=_attn_scratch(0, groups),
        compiler_params=pltpu.CompilerParams(
            dimension_semantics=("parallel", "parallel", "arbitrary"),
            vmem_limit_bytes=VMEM_LIMIT_BYTES),
        name="diff_attention",
    )(qk, qk, vt, lq1, lk1, lq2, lk2, subln)


def _out_ffn_body(x_ref, mo_ref, do_ref, wo_ref, gain_ref, wg_ref, wu_ref, wd_ref, y_ref, *, width):
    x = x_ref[...]
    att = (jnp.dot(mo_ref[...], wo_ref[0:width, :], preferred_element_type=_F32)
           + jnp.dot(do_ref[...], wo_ref[width:2 * width, :], preferred_element_type=_F32))
    x1 = x + att
    ms = jnp.mean(x1 * x1, axis=-1, keepdims=True)
    h = (x1 * lax.rsqrt(ms + RMS_EPS) * gain_ref[...]).astype(_BF16)
    g = jnp.dot(h, wg_ref[...], preferred_element_type=_F32)
    u = jnp.dot(h, wu_ref[...], preferred_element_type=_F32)
    a = (g * jax.nn.sigmoid(g) * u).astype(_BF16)
    y_ref[...] = x1 + jnp.dot(a, wd_ref[...], preferred_element_type=_F32)


def _out_ffn(x2, mo2, do2, wo, gain, wg, wu, wd, *, tm):
    T, D = x2.shape
    width = mo2.shape[1]
    F = wg.shape[1]
    body = functools.partial(_out_ffn_body, width=width)
    row = lambda n: pl.BlockSpec((tm, n), lambda t: (t, 0))
    return pl.pallas_call(
        body,
        grid=(T // tm,),
        in_specs=[row(D), row(width), row(width), _const_spec((2 * width, D)), _const_spec((1, D)),
                  _const_spec((D, F)), _const_spec((D, F)), _const_spec((F, D))],
        out_specs=row(D),
        out_shape=jax.ShapeDtypeStruct((T, D), _F32),
        compiler_params=pltpu.CompilerParams(
            dimension_semantics=("parallel",), vmem_limit_bytes=VMEM_LIMIT_BYTES),
        name="outproj_swiglu",
    )(x2, mo2, do2, wo, gain, wg, wu, wd)


def _rope_tables(seq):
    inv = 1.0 / (ROPE_THETA ** (jnp.arange(0, HEAD_DIM, 2, dtype=_F32) / HEAD_DIM))
    ang = jnp.arange(seq, dtype=_F32)[:, None] * inv[None, :]
    cos = jnp.tile(jnp.cos(ang), (1, 2 * LANES // HEAD_DIM))
    sin = jnp.sin(ang)
    sin = jnp.tile(jnp.concatenate([-sin, sin], axis=-1), (1, LANES // HEAD_DIM))
    return cos, sin


def kernel(x, attn_norm, w_in, moba_q_norm, moba_k_norm, diff_q_norm, diff_k_norm,
           lambda_q1, lambda_k1, lambda_q2, lambda_k2, diff_subln, w_out,
           ffn_norm, w_gate, w_up, w_down):
    B, S, D = x.shape
    depth = w_in.shape[0]
    width = w_in.shape[2] // 6
    heads_per_group = width // HEAD_DIM
    tm_in = 512 if S % 512 == 0 else ATTN_BLOCK
    tm_ffn = 512 if (B * S) % 512 == 0 else ATTN_BLOCK
    cos, sin = _rope_tables(S)
    gmat = jnp.kron(jnp.eye(heads_per_group, dtype=_F32),
                    jnp.full((HEAD_DIM, HEAD_DIM), 1.0 / HEAD_DIM, _F32)).astype(_BF16)
    scale = HEAD_DIM ** -0.5 * math.log2(math.e)
    for l in range(depth):
        lambda_init = 0.8 - 0.6 * math.exp(-0.3 * l)
        tile = lambda g: jnp.tile(g.astype(_F32), heads_per_group)
        hgain = jnp.stack([tile(moba_q_norm[l]) * scale, tile(moba_k_norm[l]),
                           tile(diff_q_norm[l]) * scale, tile(diff_k_norm[l])])
        qk, vt, kmean = _inproj(x, attn_norm[l][None, :], w_in[l].astype(_BF16), gmat, hgain,
                                cos, sin, tm=tm_in)
        kmean = kmean.reshape(B, S // MOBA_BLOCK, width)
        moba_out = _moba_attention(qk, vt, kmean, width=width, groups=ATTN_GROUPS)
        diff_out = _diff_attention(qk, vt, lambda_q1[l][None, :], lambda_k1[l][None, :],
                                   lambda_q2[l][None, :], lambda_k2[l][None, :],
                                   diff_subln[l][None, :], width=width, lambda_init=lambda_init,
                                   groups=ATTN_GROUPS)
        y = _out_ffn(x.reshape(B * S, D), moba_out.reshape(B * S, width),
                     diff_out.reshape(B * S, width), w_out[l].astype(_BF16), ffn_norm[l][None, :],
                     w_gate[l].astype(_BF16), w_up[l].astype(_BF16), w_down[l].astype(_BF16),
                     tm=tm_ffn)
        x = y.reshape(B, S, D)
    return x
```

```python
import functools
import math

import jax
import jax.numpy as jnp
from jax import lax
from jax.experimental import pallas as pl
from jax.experimental.pallas import tpu as pltpu

HEAD_DIM = 64
LANES = 128
SUBLANES = 8
BF16_ROWS = 16
MOBA_BLOCK = 256
MOBA_TOPK = 3
ROPE_THETA = 10000.0
RMS_EPS = 1e-6
ATTN_BLOCK = MOBA_BLOCK
ATTN_GROUPS = 4
GROUP_COLS = 2 * ATTN_BLOCK
NEG_INIT = -1e30
VMEM_LIMIT_BYTES = 56 * 1024 * 1024

_NT = (((1,), (1,)), ((), ()))
_F32 = jnp.float32
_BF16 = jnp.bfloat16


def _const_spec(shape):
    zeros = (0,) * len(shape)
    return pl.BlockSpec(shape, lambda *_: zeros, pipeline_mode=pl.Buffered(1))


def _inproj_body(x_ref, gain_ref, w_ref, gmat_ref, hgain_ref, cos_ref, sin_ref,
                 qk_ref, vt_ref, kmean_ref, *, qk_cols, v_cols, width):
    x = x_ref[0]
    ms = jnp.mean(x * x, axis=-1, keepdims=True)
    h = (x * lax.rsqrt(ms + RMS_EPS) * gain_ref[...]).astype(_BF16)
    proj = jnp.dot(h, w_ref[...], preferred_element_type=_F32)

    tm = x.shape[0]
    cos = cos_ref[...]
    sin = sin_ref[...]
    lane = lax.broadcasted_iota(jnp.int32, (tm, LANES), 1)
    first_half = (lane % HEAD_DIM) < (HEAD_DIM // 2)

    for gi, c0 in enumerate(qk_cols):
        p = proj[:, c0:c0 + width]
        msq = jnp.dot((p * p).astype(_BF16), gmat_ref[...], preferred_element_type=_F32)
        y = p * lax.rsqrt(msq + RMS_EPS) * hgain_ref[gi:gi + 1, :]
        for cc in range(width // LANES):
            yc = y[:, cc * LANES:(cc + 1) * LANES]
            rot = jnp.where(first_half, pltpu.roll(yc, LANES - HEAD_DIM // 2, 1),
                            pltpu.roll(yc, HEAD_DIM // 2, 1))
            oc = yc * cos + rot * sin
            qk_ref[0, :, gi * width + cc * LANES:gi * width + (cc + 1) * LANES] = oc.astype(_BF16)
            if gi == 1:
                for rb in range(tm // MOBA_BLOCK):
                    kmean_ref[0, 0, rb:rb + 1, cc * LANES:(cc + 1) * LANES] = jnp.mean(
                        oc[rb * MOBA_BLOCK:(rb + 1) * MOBA_BLOCK], axis=0, keepdims=True)

    for vi, c0 in enumerate(v_cols):
        for rb in range(tm // ATTN_BLOCK):
            v = proj[rb * ATTN_BLOCK:(rb + 1) * ATTN_BLOCK, c0:c0 + width]
            vt_ref[0, rb, vi * width:(vi + 1) * width, :] = v.T.astype(_BF16)


def _inproj(x, gain, w, gmat, hgain, cos, sin, *, tm):
    B, S, D = x.shape
    n_out = w.shape[1]
    width = n_out // 6
    qk_cols = (0, width, 3 * width, 4 * width)
    v_cols = (2 * width, 5 * width)
    body = functools.partial(_inproj_body, qk_cols=qk_cols, v_cols=v_cols, width=width)
    return pl.pallas_call(
        body,
        grid=(B, S // tm),
        in_specs=[
            pl.BlockSpec((1, tm, D), lambda b, t: (b, t, 0)),
            _const_spec((1, D)),
            _const_spec((D, n_out)),
            _const_spec((width, width)),
            _const_spec((4, width)),
            pl.BlockSpec((tm, LANES), lambda b, t: (t, 0)),
            pl.BlockSpec((tm, LANES), lambda b, t: (t, 0)),
        ],
        out_specs=[
            pl.BlockSpec((1, tm, 4 * width), lambda b, t: (b, t, 0)),
            pl.BlockSpec((1, tm // ATTN_BLOCK, 2 * width, ATTN_BLOCK), lambda b, t: (b, t, 0, 0)),
            pl.BlockSpec((1, 1, tm // MOBA_BLOCK, width), lambda b, t: (b, t, 0, 0)),
        ],
        out_shape=[
            jax.ShapeDtypeStruct((B, S, 4 * width), _BF16),
            jax.ShapeDtypeStruct((B, S // ATTN_BLOCK, 2 * width, ATTN_BLOCK), _BF16),
            jax.ShapeDtypeStruct((B, S // tm, tm // MOBA_BLOCK, width), _F32),
        ],
        compiler_params=pltpu.CompilerParams(
            dimension_semantics=("parallel", "parallel"), vmem_limit_bytes=VMEM_LIMIT_BYTES),
        name="inproj_qknorm_rope",
    )(x, gain, w, gmat, hgain, cos, sin)


def _stacked_queries(q_ref, g):
    q = q_ref[0, :, g * LANES:(g + 1) * LANES]
    lane = lax.broadcasted_iota(jnp.int32, q.shape, 1)
    zero = jnp.zeros_like(q)
    return jnp.concatenate([jnp.where(lane < HEAD_DIM, q, zero),
                            jnp.where(lane >= HEAD_DIM, q, zero)], axis=0)


def _flash_blocks(i, qcats, k_ref, vt_ref, s_refs, p_ref, a_ref, m_ref, l_ref, acc_ref,
                  selrow_fn=None, split_heads=False):
    groups = len(qcats)

    def scores(j, slot):
        kb = k_ref[0, pl.ds(pl.multiple_of(j * ATTN_BLOCK, ATTN_BLOCK), ATTN_BLOCK), :]
        for g in range(groups):
            s_refs[slot][:, g * GROUP_COLS:(g + 1) * GROUP_COLS] = lax.dot_general(
                kb[:, g * LANES:(g + 1) * LANES], qcats[g], _NT, preferred_element_type=_F32)

    def softmax(j, slot, diag):
        s_ref = s_refs[slot]
        cols = s_ref.shape[1]

        def rows(c, n):
            s = s_ref[c * n:(c + 1) * n, :]
            if diag:
                kpos = c * n + lax.broadcasted_iota(jnp.int32, s.shape, 0)
                qpos = lax.broadcasted_iota(jnp.int32, s.shape, 1) % ATTN_BLOCK
                s = jnp.where(kpos <= qpos, s, -jnp.inf)
            return s

        mx = rows(0, SUBLANES)
        for c in range(1, ATTN_BLOCK // SUBLANES):
            mx = jnp.maximum(mx, rows(c, SUBLANES))
        bmax = jnp.max(mx, axis=0, keepdims=True)
        m_old = m_ref[...]
        if diag or selrow_fn is None:
            m_new = jnp.maximum(m_old, bmax)
            m_used = m_new
        else:
            selrow = selrow_fn(j)
            m_new = jnp.where(selrow, jnp.maximum(m_old, bmax), m_old)
            m_used = jnp.where(selrow, m_new, jnp.inf)
        alpha = jnp.exp2(m_old - m_new)
        m_ref[...] = m_new
        a_ref[slot] = alpha
        psum = jnp.zeros((BF16_ROWS, cols), _F32)
        for c in range(ATTN_BLOCK // BF16_ROWS):
            p = jnp.exp2(rows(c, BF16_ROWS) - m_used)
            psum = psum + p
            p_ref[slot, c * BF16_ROWS:(c + 1) * BF16_ROWS, :] = p.astype(_BF16)
        l_ref[...] = alpha * l_ref[...] + jnp.sum(psum, axis=0, keepdims=True)

    def values(j, slot):
        alpha = a_ref[slot]
        for g in range(groups):
            c0 = g * GROUP_COLS
            vt = vt_ref[0, j, g * LANES:(g + 1) * LANES, :]
            if split_heads:
                for h in range(2):
                    cols = slice(c0 + h * ATTN_BLOCK, c0 + (h + 1) * ATTN_BLOCK)
                    acc_ref[g, :, h * ATTN_BLOCK:(h + 1) * ATTN_BLOCK] = (
                        alpha[:, cols] * acc_ref[g, :, h * ATTN_BLOCK:(h + 1) * ATTN_BLOCK]
                        + jnp.dot(vt[h * HEAD_DIM:(h + 1) * HEAD_DIM], p_ref[slot, :, cols],
                                  preferred_element_type=_F32))
            else:
                cols = slice(c0, c0 + GROUP_COLS)
                acc_ref[g] = alpha[:, cols] * acc_ref[g] + jnp.dot(
                    vt, p_ref[slot, :, cols], preferred_element_type=_F32)

    m_ref[...] = jnp.full(m_ref.shape, NEG_INIT, _F32)
    l_ref[...] = jnp.zeros(l_ref.shape, _F32)
    acc_ref[...] = jnp.zeros(acc_ref.shape, _F32)
    a_ref[1] = jnp.ones(a_ref.shape[1:], _F32)
    p_ref[1] = jnp.zeros(p_ref.shape[1:], _BF16)
    scores(0, 0)

    def block_pair(t, carry):
        j = 2 * t
        scores(j + 1, 1)
        values(jnp.maximum(j - 1, 0), 1)
        softmax(j, 0, False)
        scores(j + 2, 0)
        values(j, 0)
        softmax(j + 1, 1, False)
        return carry

    lax.fori_loop(0, lax.shift_right_logical(i, 1), block_pair, 0)
    odd = (i & 1) == 1

    @pl.when(odd)
    def _():
        scores(i, 1)
        values(jnp.maximum(i - 2, 0), 1)
        softmax(i - 1, 0, False)
        values(i - 1, 0)
        softmax(i, 1, True)
        values(i, 1)

    @pl.when(jnp.logical_not(odd))
    def _():
        values(jnp.maximum(i - 1, 0), 1)
        softmax(i, 0, True)
        values(i, 0)


def _attn_scratch(nb_sel, groups, acc_rows):
    cols = groups * GROUP_COLS
    shapes = [pltpu.VMEM((ATTN_BLOCK, cols), _F32), pltpu.VMEM((ATTN_BLOCK, cols), _F32),
              pltpu.VMEM((2, ATTN_BLOCK, cols), _BF16),
              pltpu.VMEM((2, 1, cols), _F32)]
    if nb_sel:
        shapes.append(pltpu.VMEM((nb_sel, cols), _F32))
    shapes += [pltpu.VMEM((1, cols), _F32), pltpu.VMEM((1, cols), _F32),
               pltpu.VMEM((groups, acc_rows, GROUP_COLS), _F32)]
    return shapes


def _moba_body(q_ref, k_ref, vt_ref, kmean_ref, o_ref, s0_ref, s1_ref, p_ref, a_ref, sel_ref, m_ref,
               l_ref, acc_ref, *, nb, groups):
    i = pl.program_id(2)
    blk = lax.broadcasted_iota(jnp.int32, (nb, GROUP_COLS), 0).astype(_F32)
    valid = blk < i.astype(_F32)
    qcats = []
    for g in range(groups):
        qcat = _stacked_queries(q_ref, g)
        qcats.append(qcat)
        kmean = kmean_ref[0, :, g * LANES:(g + 1) * LANES].astype(_BF16)
        gate = lax.dot_general(kmean, qcat, _NT, preferred_element_type=_F32)
        gate = jnp.where(valid, gate, -jnp.inf)
        sel = jnp.zeros(gate.shape, _F32)
        for _ in range(MOBA_TOPK):
            mx = jnp.max(gate, axis=0, keepdims=True)
            idx = jnp.min(jnp.where(gate == mx, blk, float(nb)), axis=0, keepdims=True)
            pick = (blk == idx) & valid
            sel = jnp.where(pick, 1.0, sel)
            gate = jnp.where(pick, -jnp.inf, gate)
        sel_ref[:, g * GROUP_COLS:(g + 1) * GROUP_COLS] = sel

    blk_all = lax.broadcasted_iota(jnp.int32, sel_ref.shape, 0)

    def selrow_fn(j):
        return jnp.max(jnp.where(blk_all == j, sel_ref[...], 0.0), axis=0, keepdims=True) > 0.0

    _flash_blocks(i, qcats, k_ref, vt_ref, (s0_ref, s1_ref), p_ref, a_ref, m_ref, l_ref, acc_ref,
                  selrow_fn, split_heads=True)

    for g in range(groups):
        acc = acc_ref[g]
        l = l_ref[:, g * GROUP_COLS:(g + 1) * GROUP_COLS]
        o_t = jnp.concatenate(
            [acc[:, :ATTN_BLOCK] / l[:, :ATTN_BLOCK],
             acc[:, ATTN_BLOCK:] / l[:, ATTN_BLOCK:]], axis=0)
        o_ref[0, :, g * LANES:(g + 1) * LANES] = o_t.T.astype(o_ref.dtype)


def _moba_attention(qk, vt, kmean, *, width, groups):
    B, S, _ = qk.shape
    nb = S // ATTN_BLOCK
    gw = groups * LANES
    chunks = width // gw
    body = functools.partial(_moba_body, nb=nb, groups=groups)
    return pl.pallas_call(
        body,
        grid=(B, chunks, nb),
        in_specs=[
            pl.BlockSpec((1, ATTN_BLOCK, gw), lambda b, p, i: (b, i, p)),
            pl.BlockSpec((1, S, gw), lambda b, p, i: (b, 0, chunks + p)),
            pl.BlockSpec((1, nb, gw, ATTN_BLOCK), lambda b, p, i: (b, 0, p, 0)),
            pl.BlockSpec((1, nb, gw), lambda b, p, i: (b, 0, p)),
        ],
        out_specs=pl.BlockSpec((1, ATTN_BLOCK, gw), lambda b, p, i: (b, i, p)),
        out_shape=jax.ShapeDtypeStruct((B, S, width), _BF16),
        scratch_shapes=_attn_scratch(nb, groups, HEAD_DIM),
        compiler_params=pltpu.CompilerParams(
            dimension_semantics=("parallel", "parallel", "arbitrary"),
            vmem_limit_bytes=VMEM_LIMIT_BYTES),
        name="moba_attention",
    )(qk, qk, vt, kmean)


def _diff_body(q_ref, k_ref, vt_ref, lq1_ref, lk1_ref, lq2_ref, lk2_ref, subln_ref, o_ref,
               s0_ref, s1_ref, p_ref, a_ref, m_ref, l_ref, acc_ref, *, lambda_init, groups):
    i = pl.program_id(2)
    qcats = [_stacked_queries(q_ref, g) for g in range(groups)]
    _flash_blocks(i, qcats, k_ref, vt_ref, (s0_ref, s1_ref), p_ref, a_ref, m_ref, l_ref, acc_ref)

    lam = (jnp.exp(jnp.sum(lq1_ref[...] * lk1_ref[...], axis=-1, keepdims=True))
           - jnp.exp(jnp.sum(lq2_ref[...] * lk2_ref[...], axis=-1, keepdims=True))
           + lambda_init)
    for g in range(groups):
        acc = acc_ref[g]
        l = l_ref[:, g * GROUP_COLS:(g + 1) * GROUP_COLS]
        a_t = (acc[:, :ATTN_BLOCK] / l[:, :ATTN_BLOCK]
               - lam * (acc[:, ATTN_BLOCK:] / l[:, ATTN_BLOCK:]))
        a = a_t.T
        ms = jnp.mean(a * a, axis=-1, keepdims=True)
        out = (a * lax.rsqrt(ms + RMS_EPS) * subln_ref[...]) * (1.0 - lambda_init)
        o_ref[0, :, g * LANES:(g + 1) * LANES] = out.astype(o_ref.dtype)


def _diff_attention(qk, vt, lq1, lk1, lq2, lk2, subln, *, width, lambda_init, groups):
    B, S, _ = qk.shape
    nb = S // ATTN_BLOCK
    gw = groups * LANES
    chunks = width // gw
    body = functools.partial(_diff_body, lambda_init=lambda_init, groups=groups)
    vec = lambda n: _const_spec((1, n))
    return pl.pallas_call(
        body,
        grid=(B, chunks, nb),
        in_specs=[
            pl.BlockSpec((1, ATTN_BLOCK, gw), lambda b, h, i: (b, i, 2 * chunks + h)),
            pl.BlockSpec((1, S, gw), lambda b, h, i: (b, 0, 3 * chunks + h)),
            pl.BlockSpec((1, nb, gw, ATTN_BLOCK), lambda b, h, i: (b, 0, chunks + h, 0)),
            vec(HEAD_DIM), vec(HEAD_DIM), vec(HEAD_DIM), vec(HEAD_DIM), vec(LANES),
        ],
        out_specs=pl.BlockSpec((1, ATTN_BLOCK, gw), lambda b, h, i: (b, i, h)),
        out_shape=jax.ShapeDtypeStruct((B, S, width), _BF16),
        scratch_shapes=_attn_scratch(0, groups, LANES),
        compiler_params=pltpu.CompilerParams(
            dimension_semantics=("parallel", "parallel", "arbitrary"),
            vmem_limit_bytes=VMEM_LIMIT_BYTES),
        name="diff_attention",
    )(qk, qk, vt, lq1, lk1, lq2, lk2, subln)


def _out_ffn_body(x_ref, mo_ref, do_ref, wo_ref, gain_ref, wg_ref, wu_ref, wd_ref, y_ref, *, width):
    x = x_ref[...]
    att = (jnp.dot(mo_ref[...], wo_ref[0:width, :], preferred_element_type=_F32)
           + jnp.dot(do_ref[...], wo_ref[width:2 * width, :], preferred_element_type=_F32))
    x1 = x + att
    ms = jnp.mean(x1 * x1, axis=-1, keepdims=True)
    h = (x1 * lax.rsqrt(ms + RMS_EPS) * gain_ref[...]).astype(_BF16)
    g = jnp.dot(h, wg_ref[...], preferred_element_type=_F32)
    u = jnp.dot(h, wu_ref[...], preferred_element_type=_F32)
    a = (g * jax.nn.sigmoid(g) * u).astype(_BF16)
    y_ref[...] = x1 + jnp.dot(a, wd_ref[...], preferred_element_type=_F32)


def _out_ffn(x2, mo2, do2, wo, gain, wg, wu, wd, *, tm):
    T, D = x2.shape
    width = mo2.shape[1]
    F = wg.shape[1]
    body = functools.partial(_out_ffn_body, width=width)
    row = lambda n: pl.BlockSpec((tm, n), lambda t: (t, 0))
    return pl.pallas_call(
        body,
        grid=(T // tm,),
        in_specs=[row(D), row(width), row(width), _const_spec((2 * width, D)), _const_spec((1, D)),
                  _const_spec((D, F)), _const_spec((D, F)), _const_spec((F, D))],
        out_specs=row(D),
        out_shape=jax.ShapeDtypeStruct((T, D), _F32),
        compiler_params=pltpu.CompilerParams(
            dimension_semantics=("parallel",), vmem_limit_bytes=VMEM_LIMIT_BYTES),
        name="outproj_swiglu",
    )(x2, mo2, do2, wo, gain, wg, wu, wd)


def _rope_tables(seq):
    inv = 1.0 / (ROPE_THETA ** (jnp.arange(0, HEAD_DIM, 2, dtype=_F32) / HEAD_DIM))
    ang = jnp.arange(seq, dtype=_F32)[:, None] * inv[None, :]
    cos = jnp.tile(jnp.cos(ang), (1, 2 * LANES // HEAD_DIM))
    sin = jnp.sin(ang)
    sin = jnp.tile(jnp.concatenate([-sin, sin], axis=-1), (1, LANES // HEAD_DIM))
    return cos, sin


def kernel(x, attn_norm, w_in, moba_q_norm, moba_k_norm, diff_q_norm, diff_k_norm,
           lambda_q1, lambda_k1, lambda_q2, lambda_k2, diff_subln, w_out,
           ffn_norm, w_gate, w_up, w_down):
    B, S, D = x.shape
    depth = w_in.shape[0]
    width = w_in.shape[2] // 6
    heads_per_group = width // HEAD_DIM
    tm_in = 512 if S % 512 == 0 else ATTN_BLOCK
    tm_ffn = 512 if (B * S) % 512 == 0 else ATTN_BLOCK
    cos, sin = _rope_tables(S)
    gmat = jnp.kron(jnp.eye(heads_per_group, dtype=_F32),
                    jnp.full((HEAD_DIM, HEAD_DIM), 1.0 / HEAD_DIM, _F32)).astype(_BF16)
    scale = HEAD_DIM ** -0.5 * math.log2(math.e)
    for l in range(depth):
        lambda_init = 0.8 - 0.6 * math.exp(-0.3 * l)
        tile = lambda g: jnp.tile(g.astype(_F32), heads_per_group)
        hgain = jnp.stack([tile(moba_q_norm[l]) * scale, tile(moba_k_norm[l]),
                           tile(diff_q_norm[l]) * scale, tile(diff_k_norm[l])])
        qk, vt, kmean = _inproj(x, attn_norm[l][None, :], w_in[l].astype(_BF16), gmat, hgain,
                                cos, sin, tm=tm_in)
        kmean = kmean.reshape(B, S // MOBA_BLOCK, width)
        moba_out = _moba_attention(qk, vt, kmean, width=width, groups=ATTN_GROUPS)
        diff_out = _diff_attention(qk, vt, lambda_q1[l][None, :], lambda_k1[l][None, :],
                                   lambda_q2[l][None, :], lambda_k2[l][None, :],
                                   diff_subln[l][None, :], width=width, lambda_init=lambda_init,
                                   groups=ATTN_GROUPS)
        y = _out_ffn(x.reshape(B * S, D), moba_out.reshape(B * S, width),
                     diff_out.reshape(B * S, width), w_out[l].astype(_BF16), ffn_norm[l][None, :],
                     w_gate[l].astype(_BF16), w_up[l].astype(_BF16), w_down[l].astype(_BF16),
                     tm=tm_ffn)
        x = y.reshape(B, S, D)
    return x
```

```python
import functools
import math

import jax
import jax.numpy as jnp
from jax import lax
from jax.experimental import pallas as pl
from jax.experimental.pallas import tpu as pltpu

HEAD_DIM = 64
LANES = 128
SUBLANES = 8
BF16_ROWS = 16
MOBA_BLOCK = 256
MOBA_TOPK = 3
ROPE_THETA = 10000.0
RMS_EPS = 1e-6
ATTN_BLOCK = MOBA_BLOCK
ATTN_GROUPS = 4
GROUP_COLS = 2 * ATTN_BLOCK
NEG_INIT = -1e30
VMEM_LIMIT_BYTES = 56 * 1024 * 1024

_NT = (((1,), (1,)), ((), ()))
_F32 = jnp.float32
_BF16 = jnp.bfloat16


def _const_spec(shape):
    zeros = (0,) * len(shape)
    return pl.BlockSpec(shape, lambda *_: zeros, pipeline_mode=pl.Buffered(1))


def _inproj_body(x_ref, gain_ref, w_ref, gmat_ref, hgain_ref, cos_ref, sin_ref,
                 qk_ref, vt_ref, kmean_ref, *, qk_cols, v_cols, width):
    x = x_ref[0]
    ms = jnp.mean(x * x, axis=-1, keepdims=True)
    h = (x * lax.rsqrt(ms + RMS_EPS) * gain_ref[...]).astype(_BF16)
    proj = jnp.dot(h, w_ref[...], preferred_element_type=_F32)

    tm = x.shape[0]
    cos = cos_ref[...]
    sin = sin_ref[...]
    lane = lax.broadcasted_iota(jnp.int32, (tm, LANES), 1)
    first_half = (lane % HEAD_DIM) < (HEAD_DIM // 2)

    for gi, c0 in enumerate(qk_cols):
        p = proj[:, c0:c0 + width]
        msq = jnp.dot((p * p).astype(_BF16), gmat_ref[...], preferred_element_type=_F32)
        y = p * lax.rsqrt(msq + RMS_EPS) * hgain_ref[gi:gi + 1, :]
        for cc in range(width // LANES):
            yc = y[:, cc * LANES:(cc + 1) * LANES]
            rot = jnp.where(first_half, pltpu.roll(yc, LANES - HEAD_DIM // 2, 1),
                            pltpu.roll(yc, HEAD_DIM // 2, 1))
            oc = yc * cos + rot * sin
            qk_ref[0, :, gi * width + cc * LANES:gi * width + (cc + 1) * LANES] = oc.astype(_BF16)
            if gi == 1:
                for rb in range(tm // MOBA_BLOCK):
                    kmean_ref[0, 0, rb:rb + 1, cc * LANES:(cc + 1) * LANES] = jnp.mean(
                        oc[rb * MOBA_BLOCK:(rb + 1) * MOBA_BLOCK], axis=0, keepdims=True)

    for vi, c0 in enumerate(v_cols):
        for rb in range(tm // ATTN_BLOCK):
            v = proj[rb * ATTN_BLOCK:(rb + 1) * ATTN_BLOCK, c0:c0 + width]
            vt_ref[0, rb, vi * width:(vi + 1) * width, :] = v.T.astype(_BF16)


def _inproj(x, gain, w, gmat, hgain, cos, sin, *, tm):
    B, S, D = x.shape
    n_out = w.shape[1]
    width = n_out // 6
    qk_cols = (0, width, 3 * width, 4 * width)
    v_cols = (2 * width, 5 * width)
    body = functools.partial(_inproj_body, qk_cols=qk_cols, v_cols=v_cols, width=width)
    return pl.pallas_call(
        body,
        grid=(B, S // tm),
        in_specs=[
            pl.BlockSpec((1, tm, D), lambda b, t: (b, t, 0)),
            _const_spec((1, D)),
            _const_spec((D, n_out)),
            _const_spec((width, width)),
            _const_spec((4, width)),
            pl.BlockSpec((tm, LANES), lambda b, t: (t, 0)),
            pl.BlockSpec((tm, LANES), lambda b, t: (t, 0)),
        ],
        out_specs=[
            pl.BlockSpec((1, tm, 4 * width), lambda b, t: (b, t, 0)),
            pl.BlockSpec((1, tm // ATTN_BLOCK, 2 * width, ATTN_BLOCK), lambda b, t: (b, t, 0, 0)),
            pl.BlockSpec((1, 1, tm // MOBA_BLOCK, width), lambda b, t: (b, t, 0, 0)),
        ],
        out_shape=[
            jax.ShapeDtypeStruct((B, S, 4 * width), _BF16),
            jax.ShapeDtypeStruct((B, S // ATTN_BLOCK, 2 * width, ATTN_BLOCK), _BF16),
            jax.ShapeDtypeStruct((B, S // tm, tm // MOBA_BLOCK, width), _F32),
        ],
        compiler_params=pltpu.CompilerParams(
            dimension_semantics=("parallel", "parallel"), vmem_limit_bytes=VMEM_LIMIT_BYTES),
        name="inproj_qknorm_rope",
    )(x, gain, w, gmat, hgain, cos, sin)


def _stacked_queries(q_ref, g):
    q = q_ref[0, :, g * LANES:(g + 1) * LANES]
    lane = lax.broadcasted_iota(jnp.int32, q.shape, 1)
    zero = jnp.zeros_like(q)
    return jnp.concatenate([jnp.where(lane < HEAD_DIM, q, zero),
                            jnp.where(lane >= HEAD_DIM, q, zero)], axis=0)


def _flash_blocks(i, qcats, k_ref, vt_ref, s_refs, p_ref, a_ref, m_ref, acc_ref,
                  selrow_fn=None, split_heads=False):
    groups = len(qcats)
    ones = jnp.ones((BF16_ROWS, ATTN_BLOCK), _BF16)

    def scores(j, slot):
        kb = k_ref[0, pl.ds(pl.multiple_of(j * ATTN_BLOCK, ATTN_BLOCK), ATTN_BLOCK), :]
        for g in range(groups):
            s_refs[slot][:, g * GROUP_COLS:(g + 1) * GROUP_COLS] = lax.dot_general(
                kb[:, g * LANES:(g + 1) * LANES], qcats[g], _NT,
                preferred_element_type=_F32).astype(_BF16)

    def softmax(j, slot, diag):
        s_ref = s_refs[slot]
        cols = s_ref.shape[1]
        n_chunks = ATTN_BLOCK // BF16_ROWS

        def rows(c):
            s = s_ref[c * BF16_ROWS:(c + 1) * BF16_ROWS, :]
            if diag:
                kpos = c * BF16_ROWS + lax.broadcasted_iota(jnp.int32, s.shape, 0)
                qpos = lax.broadcasted_iota(jnp.int32, s.shape, 1) % ATTN_BLOCK
                s = jnp.where(kpos <= qpos, s.astype(_F32), -jnp.inf)
            return s

        mx = rows(0)
        for c in range(1, n_chunks):
            mx = jnp.maximum(mx, rows(c))
        bmax = jnp.max(mx.astype(_F32), axis=0, keepdims=True)
        m_old = m_ref[...]
        if diag or selrow_fn is None:
            m_new = jnp.maximum(m_old, bmax)
            m_used = m_new
        else:
            selrow = selrow_fn(j)
            m_new = jnp.where(selrow, jnp.maximum(m_old, bmax), m_old)
            m_used = jnp.where(selrow, m_new, jnp.inf)
        m_ref[...] = m_new
        a_ref[slot] = jnp.exp2(m_old - m_new)
        m_tile = jnp.broadcast_to(m_used, (BF16_ROWS, cols))
        if not diag:
            m_tile = m_tile.astype(_BF16)
        for c in range(n_chunks):
            p_ref[slot, c * BF16_ROWS:(c + 1) * BF16_ROWS, :] = jnp.exp2(rows(c) - m_tile).astype(_BF16)

    def values(j, slot):
        alpha = a_ref[slot]
        for g in range(groups):
            c0 = g * GROUP_COLS
            vt = vt_ref[0, j, g * LANES:(g + 1) * LANES, :]
            if split_heads:
                for h in range(2):
                    cols = slice(c0 + h * ATTN_BLOCK, c0 + (h + 1) * ATTN_BLOCK)
                    lhs = jnp.concatenate([vt[h * HEAD_DIM:(h + 1) * HEAD_DIM], ones], axis=0)
                    acc_ref[g, :, h * ATTN_BLOCK:(h + 1) * ATTN_BLOCK] = (
                        alpha[:, cols] * acc_ref[g, :, h * ATTN_BLOCK:(h + 1) * ATTN_BLOCK]
                        + jnp.dot(lhs, p_ref[slot, :, cols], preferred_element_type=_F32))
            else:
                cols = slice(c0, c0 + GROUP_COLS)
                lhs = jnp.concatenate([vt, ones], axis=0)
                acc_ref[g] = alpha[:, cols] * acc_ref[g] + jnp.dot(
                    lhs, p_ref[slot, :, cols], preferred_element_type=_F32)

    m_ref[...] = jnp.full(m_ref.shape, NEG_INIT, _F32)
    acc_ref[...] = jnp.zeros(acc_ref.shape, _F32)
    a_ref[1] = jnp.ones(a_ref.shape[1:], _F32)
    p_ref[1] = jnp.zeros(p_ref.shape[1:], _BF16)
    scores(0, 0)

    def block_pair(t, carry):
        j = 2 * t
        scores(j + 1, 1)
        values(jnp.maximum(j - 1, 0), 1)
        softmax(j, 0, False)
        scores(j + 2, 0)
        values(j, 0)
        softmax(j + 1, 1, False)
        return carry

    lax.fori_loop(0, lax.shift_right_logical(i, 1), block_pair, 0)
    odd = (i & 1) == 1

    @pl.when(odd)
    def _():
        scores(i, 1)
        values(jnp.maximum(i - 2, 0), 1)
        softmax(i - 1, 0, False)
        values(i - 1, 0)
        softmax(i, 1, True)
        values(i, 1)

    @pl.when(jnp.logical_not(odd))
    def _():
        values(jnp.maximum(i - 1, 0), 1)
        softmax(i, 0, True)
        values(i, 0)


def _attn_scratch(nb_sel, groups, acc_rows):
    cols = groups * GROUP_COLS
    shapes = [pltpu.VMEM((ATTN_BLOCK, cols), _BF16), pltpu.VMEM((ATTN_BLOCK, cols), _BF16),
              pltpu.VMEM((2, ATTN_BLOCK, cols), _BF16),
              pltpu.VMEM((2, 1, cols), _F32)]
    if nb_sel:
        shapes.append(pltpu.VMEM((nb_sel, cols), _F32))
    shapes += [pltpu.VMEM((1, cols), _F32),
               pltpu.VMEM((groups, acc_rows, GROUP_COLS), _F32)]
    return shapes


def _moba_body(q_ref, k_ref, vt_ref, kmean_ref, o_ref, s0_ref, s1_ref, p_ref, a_ref, sel_ref, m_ref,
               acc_ref, *, nb, groups):
    i = pl.program_id(2)
    blk = lax.broadcasted_iota(jnp.int32, (nb, GROUP_COLS), 0).astype(_F32)
    valid = blk < i.astype(_F32)
    qcats = []
    for g in range(groups):
        qcat = _stacked_queries(q_ref, g)
        qcats.append(qcat)
        kmean = kmean_ref[0, :, g * LANES:(g + 1) * LANES].astype(_BF16)
        gate = lax.dot_general(kmean, qcat, _NT, preferred_element_type=_F32)
        gate = jnp.where(valid, gate, -jnp.inf)
        sel = jnp.zeros(gate.shape, _F32)
        for _ in range(MOBA_TOPK):
            mx = jnp.max(gate, axis=0, keepdims=True)
            idx = jnp.min(jnp.where(gate == mx, blk, float(nb)), axis=0, keepdims=True)
            pick = (blk == idx) & valid
            sel = jnp.where(pick, 1.0, sel)
            gate = jnp.where(pick, -jnp.inf, gate)
        sel_ref[:, g * GROUP_COLS:(g + 1) * GROUP_COLS] = sel

    blk_all = lax.broadcasted_iota(jnp.int32, sel_ref.shape, 0)

    def selrow_fn(j):
        return jnp.max(jnp.where(blk_all == j, sel_ref[...], 0.0), axis=0, keepdims=True) > 0.0

    _flash_blocks(i, qcats, k_ref, vt_ref, (s0_ref, s1_ref), p_ref, a_ref, m_ref, acc_ref,
                  selrow_fn, split_heads=True)

    for g in range(groups):
        acc = acc_ref[g]
        o = acc[:HEAD_DIM] / acc[HEAD_DIM:HEAD_DIM + 1]
        o_t = jnp.concatenate([o[:, :ATTN_BLOCK], o[:, ATTN_BLOCK:]], axis=0)
        o_ref[0, :, g * LANES:(g + 1) * LANES] = o_t.T.astype(o_ref.dtype)


def _moba_attention(qk, vt, kmean, *, width, groups):
    B, S, _ = qk.shape
    nb = S // ATTN_BLOCK
    gw = groups * LANES
    chunks = width // gw
    body = functools.partial(_moba_body, nb=nb, groups=groups)
    return pl.pallas_call(
        body,
        grid=(B, chunks, nb),
        in_specs=[
            pl.BlockSpec((1, ATTN_BLOCK, gw), lambda b, p, i: (b, i, p)),
            pl.BlockSpec((1, S, gw), lambda b, p, i: (b, 0, chunks + p)),
            pl.BlockSpec((1, nb, gw, ATTN_BLOCK), lambda b, p, i: (b, 0, p, 0)),
            pl.BlockSpec((1, nb, gw), lambda b, p, i: (b, 0, p)),
        ],
        out_specs=pl.BlockSpec((1, ATTN_BLOCK, gw), lambda b, p, i: (b, i, p)),
        out_shape=jax.ShapeDtypeStruct((B, S, width), _BF16),
        scratch_shapes=_attn_scratch(nb, groups, HEAD_DIM + BF16_ROWS),
        compiler_params=pltpu.CompilerParams(
            dimension_semantics=("parallel", "parallel", "arbitrary"),
            vmem_limit_bytes=VMEM_LIMIT_BYTES),
        name="moba_attention",
    )(qk, qk, vt, kmean)


def _diff_body(q_ref, k_ref, vt_ref, lq1_ref, lk1_ref, lq2_ref, lk2_ref, subln_ref, o_ref,
               s0_ref, s1_ref, p_ref, a_ref, m_ref, acc_ref, *, lambda_init, groups):
    i = pl.program_id(2)
    qcats = [_stacked_queries(q_ref, g) for g in range(groups)]
    _flash_blocks(i, qcats, k_ref, vt_ref, (s0_ref, s1_ref), p_ref, a_ref, m_ref, acc_ref)

    lam = (jnp.exp(jnp.sum(lq1_ref[...] * lk1_ref[...], axis=-1, keepdims=True))
           - jnp.exp(jnp.sum(lq2_ref[...] * lk2_ref[...], axis=-1, keepdims=True))
           + lambda_init)
    for g in range(groups):
        acc = acc_ref[g]
        o = acc[:LANES] / acc[LANES:LANES + 1]
        a_t = o[:, :ATTN_BLOCK] - lam * o[:, ATTN_BLOCK:]
        a = a_t.T
        ms = jnp.mean(a * a, axis=-1, keepdims=True)
        out = (a * lax.rsqrt(ms + RMS_EPS) * subln_ref[...]) * (1.0 - lambda_init)
        o_ref[0, :, g * LANES:(g + 1) * LANES] = out.astype(o_ref.dtype)


def _diff_attention(qk, vt, lq1, lk1, lq2, lk2, subln, *, width, lambda_init, groups):
    B, S, _ = qk.shape
    nb = S // ATTN_BLOCK
    gw = groups * LANES
    chunks = width // gw
    body = functools.partial(_diff_body, lambda_init=lambda_init, groups=groups)
    vec = lambda n: _const_spec((1, n))
    return pl.pallas_call(
        body,
        grid=(B, chunks, nb),
        in_specs=[
            pl.BlockSpec((1, ATTN_BLOCK, gw), lambda b, h, i: (b, i, 2 * chunks + h)),
            pl.BlockSpec((1, S, gw), lambda b, h, i: (b, 0, 3 * chunks + h)),
            pl.BlockSpec((1, nb, gw, ATTN_BLOCK), lambda b, h, i: (b, 0, chunks + h, 0)),
            vec(HEAD_DIM), vec(HEAD_DIM), vec(HEAD_DIM), vec(HEAD_DIM), vec(LANES),
        ],
        out_specs=pl.BlockSpec((1, ATTN_BLOCK, gw), lambda b, h, i: (b, i, h)),
        out_shape=jax.ShapeDtypeStruct((B, S, width), _BF16),
        scratch_shapes=_attn_scratch(0, groups, LANES + BF16_ROWS),
        compiler_params=pltpu.CompilerParams(
            dimension_semantics=("parallel", "parallel", "arbitrary"),
            vmem_limit_bytes=VMEM_LIMIT_BYTES),
        name="diff_attention",
    )(qk, qk, vt, lq1, lk1, lq2, lk2, subln)


def _out_ffn_body(x_ref, mo_ref, do_ref, wo_ref, gain_ref, wg_ref, wu_ref, wd_ref, y_ref, *, width):
    x = x_ref[...]
    att = (jnp.dot(mo_ref[...], wo_ref[0:width, :], preferred_element_type=_F32)
           + jnp.dot(do_ref[...], wo_ref[width:2 * width, :], preferred_element_type=_F32))
    x1 = x + att
    ms = jnp.mean(x1 * x1, axis=-1, keepdims=True)
    h = (x1 * lax.rsqrt(ms + RMS_EPS) * gain_ref[...]).astype(_BF16)
    g = jnp.dot(h, wg_ref[...], preferred_element_type=_F32)
    u = jnp.dot(h, wu_ref[...], preferred_element_type=_F32)
    a = (g * jax.nn.sigmoid(g) * u).astype(_BF16)
    y_ref[...] = x1 + jnp.dot(a, wd_ref[...], preferred_element_type=_F32)


def _out_ffn(x2, mo2, do2, wo, gain, wg, wu, wd, *, tm):
    T, D = x2.shape
    width = mo2.shape[1]
    F = wg.shape[1]
    body = functools.partial(_out_ffn_body, width=width)
    row = lambda n: pl.BlockSpec((tm, n), lambda t: (t, 0))
    return pl.pallas_call(
        body,
        grid=(T // tm,),
        in_specs=[row(D), row(width), row(width), _const_spec((2 * width, D)), _const_spec((1, D)),
                  _const_spec((D, F)), _const_spec((D, F)), _const_spec((F, D))],
        out_specs=row(D),
        out_shape=jax.ShapeDtypeStruct((T, D), _F32),
        compiler_params=pltpu.CompilerParams(
            dimension_semantics=("parallel",), vmem_limit_bytes=VMEM_LIMIT_BYTES),
        name="outproj_swiglu",
    )(x2, mo2, do2, wo, gain, wg, wu, wd)


def _rope_tables(seq):
    inv = 1.0 / (ROPE_THETA ** (jnp.arange(0, HEAD_DIM, 2, dtype=_F32) / HEAD_DIM))
    ang = jnp.arange(seq, dtype=_F32)[:, None] * inv[None, :]
    cos = jnp.tile(jnp.cos(ang), (1, 2 * LANES // HEAD_DIM))
    sin = jnp.sin(ang)
    sin = jnp.tile(jnp.concatenate([-sin, sin], axis=-1), (1, LANES // HEAD_DIM))
    return cos, sin


def kernel(x, attn_norm, w_in, moba_q_norm, moba_k_norm, diff_q_norm, diff_k_norm,
           lambda_q1, lambda_k1, lambda_q2, lambda_k2, diff_subln, w_out,
           ffn_norm, w_gate, w_up, w_down):
    B, S, D = x.shape
    depth = w_in.shape[0]
    width = w_in.shape[2] // 6
    heads_per_group = width // HEAD_DIM
    tm_in = 512 if S % 512 == 0 else ATTN_BLOCK
    tm_ffn = 512 if (B * S) % 512 == 0 else ATTN_BLOCK
    cos, sin = _rope_tables(S)
    gmat = jnp.kron(jnp.eye(heads_per_group, dtype=_F32),
                    jnp.full((HEAD_DIM, HEAD_DIM), 1.0 / HEAD_DIM, _F32)).astype(_BF16)
    scale = HEAD_DIM ** -0.5 * math.log2(math.e)
    for l in range(depth):
        lambda_init = 0.8 - 0.6 * math.exp(-0.3 * l)
        tile = lambda g: jnp.tile(g.astype(_F32), heads_per_group)
        hgain = jnp.stack([tile(moba_q_norm[l]) * scale, tile(moba_k_norm[l]),
                           tile(diff_q_norm[l]) * scale, tile(diff_k_norm[l])])
        qk, vt, kmean = _inproj(x, attn_norm[l][None, :], w_in[l].astype(_BF16), gmat, hgain,
                                cos, sin, tm=tm_in)
        kmean = kmean.reshape(B, S // MOBA_BLOCK, width)
        moba_out = _moba_attention(qk, vt, kmean, width=width, groups=ATTN_GROUPS)
        diff_out = _diff_attention(qk, vt, lambda_q1[l][None, :], lambda_k1[l][None, :],
                                   lambda_q2[l][None, :], lambda_k2[l][None, :],
                                   diff_subln[l][None, :], width=width, lambda_init=lambda_init,
                                   groups=ATTN_GROUPS)
        y = _out_ffn(x.reshape(B * S, D), moba_out.reshape(B * S, width),
                     diff_out.reshape(B * S, width), w_out[l].astype(_BF16), ffn_norm[l][None, :],
                     w_gate[l].astype(_BF16), w_up[l].astype(_BF16), w_down[l].astype(_BF16),
                     tm=tm_ffn)
        x = y.reshape(B, S, D)
    return x
```

```python
import functools
import math

import jax
import jax.numpy as jnp
from jax import lax
from jax.experimental import pallas as pl
from jax.experimental.pallas import tpu as pltpu

HEAD_DIM = 64
LANES = 128
BF16_ROWS = 16
MOBA_BLOCK = 256
MOBA_TOPK = 3
ROPE_THETA = 10000.0
RMS_EPS = 1e-6
KEY_BLOCK = MOBA_BLOCK
Q_SUB = 2
Q_BLOCK = Q_SUB * MOBA_BLOCK
ATTN_GROUPS = 4
GROUP_COLS = 2 * Q_BLOCK
NEG_INIT = -1e30
VMEM_LIMIT_BYTES = 56 * 1024 * 1024

_NT = (((1,), (1,)), ((), ()))
_F32 = jnp.float32
_BF16 = jnp.bfloat16


def _const_spec(shape):
    zeros = (0,) * len(shape)
    return pl.BlockSpec(shape, lambda *_: zeros, pipeline_mode=pl.Buffered(1))


def _inproj_body(x_ref, gain_ref, w_ref, gmat_ref, hgain_ref, cos_ref, sin_ref,
                 qk_ref, vt_ref, kmean_ref, *, qk_cols, v_cols, width):
    x = x_ref[0]
    ms = jnp.mean(x * x, axis=-1, keepdims=True)
    h = (x * lax.rsqrt(ms + RMS_EPS) * gain_ref[...]).astype(_BF16)
    proj = jnp.dot(h, w_ref[...], preferred_element_type=_F32)

    tm = x.shape[0]
    cos = cos_ref[...]
    sin = sin_ref[...]
    lane = lax.broadcasted_iota(jnp.int32, (tm, LANES), 1)
    first_half = (lane % HEAD_DIM) < (HEAD_DIM // 2)

    for gi, c0 in enumerate(qk_cols):
        p = proj[:, c0:c0 + width]
        msq = jnp.dot((p * p).astype(_BF16), gmat_ref[...], preferred_element_type=_F32)
        y = p * lax.rsqrt(msq + RMS_EPS) * hgain_ref[gi:gi + 1, :]
        for cc in range(width // LANES):
            yc = y[:, cc * LANES:(cc + 1) * LANES]
            rot = jnp.where(first_half, pltpu.roll(yc, LANES - HEAD_DIM // 2, 1),
                            pltpu.roll(yc, HEAD_DIM // 2, 1))
            oc = yc * cos + rot * sin
            qk_ref[0, :, gi * width + cc * LANES:gi * width + (cc + 1) * LANES] = oc.astype(_BF16)
            if gi == 1:
                for rb in range(tm // MOBA_BLOCK):
                    kmean_ref[0, 0, rb:rb + 1, cc * LANES:(cc + 1) * LANES] = jnp.mean(
                        oc[rb * MOBA_BLOCK:(rb + 1) * MOBA_BLOCK], axis=0, keepdims=True)

    for vi, c0 in enumerate(v_cols):
        for rb in range(tm // KEY_BLOCK):
            v = proj[rb * KEY_BLOCK:(rb + 1) * KEY_BLOCK, c0:c0 + width]
            vt_ref[0, rb, vi * width:(vi + 1) * width, :] = v.T.astype(_BF16)


def _inproj(x, gain, w, gmat, hgain, cos, sin, *, tm):
    B, S, D = x.shape
    n_out = w.shape[1]
    width = n_out // 6
    qk_cols = (0, width, 3 * width, 4 * width)
    v_cols = (2 * width, 5 * width)
    body = functools.partial(_inproj_body, qk_cols=qk_cols, v_cols=v_cols, width=width)
    return pl.pallas_call(
        body,
        grid=(B, S // tm),
        in_specs=[
            pl.BlockSpec((1, tm, D), lambda b, t: (b, t, 0)),
            _const_spec((1, D)),
            _const_spec((D, n_out)),
            _const_spec((width, width)),
            _const_spec((4, width)),
            pl.BlockSpec((tm, LANES), lambda b, t: (t, 0)),
            pl.BlockSpec((tm, LANES), lambda b, t: (t, 0)),
        ],
        out_specs=[
            pl.BlockSpec((1, tm, 4 * width), lambda b, t: (b, t, 0)),
            pl.BlockSpec((1, tm // KEY_BLOCK, 2 * width, KEY_BLOCK), lambda b, t: (b, t, 0, 0)),
            pl.BlockSpec((1, 1, tm // MOBA_BLOCK, width), lambda b, t: (b, t, 0, 0)),
        ],
        out_shape=[
            jax.ShapeDtypeStruct((B, S, 4 * width), _BF16),
            jax.ShapeDtypeStruct((B, S // KEY_BLOCK, 2 * width, KEY_BLOCK), _BF16),
            jax.ShapeDtypeStruct((B, S // tm, tm // MOBA_BLOCK, width), _F32),
        ],
        compiler_params=pltpu.CompilerParams(
            dimension_semantics=("parallel", "parallel"), vmem_limit_bytes=VMEM_LIMIT_BYTES),
        name="inproj_qknorm_rope",
    )(x, gain, w, gmat, hgain, cos, sin)


def _stacked_queries(q_ref, g):
    q = q_ref[0, :, g * LANES:(g + 1) * LANES]
    lane = lax.broadcasted_iota(jnp.int32, q.shape, 1)
    zero = jnp.zeros_like(q)
    return jnp.concatenate([jnp.where(lane < HEAD_DIM, q, zero),
                            jnp.where(lane >= HEAD_DIM, q, zero)], axis=0)


def _query_sub_block(cols):
    col = lax.broadcasted_iota(jnp.int32, (1, cols), 1)
    return (col // MOBA_BLOCK) % Q_SUB


def _causal_bias():
    kpos = lax.broadcasted_iota(jnp.int32, (KEY_BLOCK, GROUP_COLS), 0)
    col = lax.broadcasted_iota(jnp.int32, (KEY_BLOCK, GROUP_COLS), 1)
    causal = kpos <= (col % MOBA_BLOCK)
    second = ((col // MOBA_BLOCK) % Q_SUB) == 1
    neg = jnp.float32(-jnp.inf)
    tiles = [jnp.zeros((KEY_BLOCK, GROUP_COLS), _F32),
             jnp.where(causal | second, 0.0, neg),
             jnp.where(causal, 0.0, neg)]
    return jnp.stack(tiles).astype(_BF16)


def _flash_blocks(step, qcats, k_ref, vt_ref, bias_ref, s_refs, p_ref, a_ref, m_ref, acc_ref,
                  active_fn, split_heads):
    groups = len(qcats)
    n_blocks = k_ref.shape[1] // KEY_BLOCK
    n_chunks = KEY_BLOCK // BF16_ROWS
    ones = jnp.ones((BF16_ROWS, KEY_BLOCK), _BF16)
    own = 2 * step

    def scores(j, slot):
        j = jnp.minimum(j, n_blocks - 1)
        kb = k_ref[0, pl.ds(pl.multiple_of(j * KEY_BLOCK, KEY_BLOCK), KEY_BLOCK), :]
        bias = bias_ref[jnp.clip(j - own + 1, 0, 2)]
        for g in range(groups):
            s = lax.dot_general(kb[:, g * LANES:(g + 1) * LANES], qcats[g], _NT,
                                preferred_element_type=_F32)
            s_refs[slot][:, g * GROUP_COLS:(g + 1) * GROUP_COLS] = s.astype(_BF16) + bias

    def softmax(j, slot):
        s_ref = s_refs[slot]
        cols = s_ref.shape[1]
        mx = s_ref[0:BF16_ROWS, :]
        for c in range(1, n_chunks):
            mx = jnp.maximum(mx, s_ref[c * BF16_ROWS:(c + 1) * BF16_ROWS, :])
        bmax = jnp.max(mx.astype(_F32), axis=0, keepdims=True)
        active = active_fn(j)
        m_old = m_ref[...]
        m_new = jnp.where(active, jnp.maximum(m_old, bmax), m_old)
        m_used = jnp.where(active, m_new, jnp.inf)
        m_ref[...] = m_new
        a_ref[slot] = jnp.exp2(m_old - m_new)
        m_tile = jnp.broadcast_to(m_used, (BF16_ROWS, cols)).astype(_BF16)
        for c in range(n_chunks):
            rows = slice(c * BF16_ROWS, (c + 1) * BF16_ROWS)
            p_ref[slot, rows, :] = jnp.exp2(s_ref[rows, :] - m_tile)

    def values(j, slot):
        alpha = a_ref[slot]
        for g in range(groups):
            c0 = g * GROUP_COLS
            vt = vt_ref[0, j, g * LANES:(g + 1) * LANES, :]
            if split_heads:
                for h in range(2):
                    cols = slice(c0 + h * Q_BLOCK, c0 + (h + 1) * Q_BLOCK)
                    lhs = jnp.concatenate([vt[h * HEAD_DIM:(h + 1) * HEAD_DIM], ones], axis=0)
                    acc_ref[g, :, h * Q_BLOCK:(h + 1) * Q_BLOCK] = (
                        alpha[:, cols] * acc_ref[g, :, h * Q_BLOCK:(h + 1) * Q_BLOCK]
                        + jnp.dot(lhs, p_ref[slot, :, cols], preferred_element_type=_F32))
            else:
                cols = slice(c0, c0 + GROUP_COLS)
                lhs = jnp.concatenate([vt, ones], axis=0)
                acc_ref[g] = alpha[:, cols] * acc_ref[g] + jnp.dot(
                    lhs, p_ref[slot, :, cols], preferred_element_type=_F32)

    m_ref[...] = jnp.full(m_ref.shape, NEG_INIT, _F32)
    acc_ref[...] = jnp.zeros(acc_ref.shape, _F32)
    a_ref[1] = jnp.ones(a_ref.shape[1:], _F32)
    p_ref[1] = jnp.zeros(p_ref.shape[1:], _BF16)
    scores(0, 0)

    def block_pair(t, carry):
        j = 2 * t
        scores(j + 1, 1)
        values(jnp.maximum(j - 1, 0), 1)
        softmax(j, 0)
        scores(j + 2, 0)
        values(j, 0)
        softmax(j + 1, 1)
        return carry

    lax.fori_loop(0, step + 1, block_pair, 0)
    values(own + 1, 1)


def _attn_scratch(nb_sel, groups, acc_rows):
    cols = groups * GROUP_COLS
    shapes = [pltpu.VMEM((KEY_BLOCK, cols), _BF16), pltpu.VMEM((KEY_BLOCK, cols), _BF16),
              pltpu.VMEM((2, KEY_BLOCK, cols), _BF16),
              pltpu.VMEM((2, 1, cols), _F32)]
    if nb_sel:
        shapes.append(pltpu.VMEM((nb_sel, cols), _F32))
    shapes += [pltpu.VMEM((1, cols), _F32),
               pltpu.VMEM((groups, acc_rows, GROUP_COLS), _F32)]
    return shapes


def _moba_body(q_ref, k_ref, vt_ref, kmean_ref, bias_ref, o_ref, s0_ref, s1_ref, p_ref, a_ref,
               sel_ref, m_ref, acc_ref, *, nb, groups):
    step = pl.program_id(2)
    blk = lax.broadcasted_iota(jnp.int32, (nb, GROUP_COLS), 0)
    own_blk = 2 * step + _query_sub_block(GROUP_COLS)
    valid = blk < own_blk
    blk_f = blk.astype(_F32)
    qcats = []
    for g in range(groups):
        qcat = _stacked_queries(q_ref, g)
        qcats.append(qcat)
        kmean = kmean_ref[0, :, g * LANES:(g + 1) * LANES].astype(_BF16)
        gate = lax.dot_general(kmean, qcat, _NT, preferred_element_type=_F32)
        gate = jnp.where(valid, gate, -jnp.inf)
        sel = jnp.where(blk == own_blk, 1.0, 0.0)
        for _ in range(MOBA_TOPK):
            mx = jnp.max(gate, axis=0, keepdims=True)
            idx = jnp.min(jnp.where(gate == mx, blk_f, float(nb)), axis=0, keepdims=True)
            pick = (blk_f == idx) & valid
            sel = jnp.where(pick, 1.0, sel)
            gate = jnp.where(pick, -jnp.inf, gate)
        sel_ref[:, g * GROUP_COLS:(g + 1) * GROUP_COLS] = sel

    def active_fn(j):
        return sel_ref[pl.ds(j, 1), :] > 0.0

    _flash_blocks(step, qcats, k_ref, vt_ref, bias_ref, (s0_ref, s1_ref), p_ref, a_ref, m_ref,
                  acc_ref, active_fn, split_heads=True)

    for g in range(groups):
        acc = acc_ref[g]
        o = acc[:HEAD_DIM] / acc[HEAD_DIM:HEAD_DIM + 1]
        o_t = jnp.concatenate([o[:, :Q_BLOCK], o[:, Q_BLOCK:]], axis=0)
        o_ref[0, :, g * LANES:(g + 1) * LANES] = o_t.T.astype(o_ref.dtype)


def _moba_attention(qk, vt, kmean, bias, *, width, groups):
    B, S, _ = qk.shape
    nb = S // KEY_BLOCK
    gw = groups * LANES
    chunks = width // gw
    body = functools.partial(_moba_body, nb=nb, groups=groups)
    return pl.pallas_call(
        body,
        grid=(B, chunks, S // Q_BLOCK),
        in_specs=[
            pl.BlockSpec((1, Q_BLOCK, gw), lambda b, p, i: (b, i, p)),
            pl.BlockSpec((1, S, gw), lambda b, p, i: (b, 0, chunks + p)),
            pl.BlockSpec((1, nb, gw, KEY_BLOCK), lambda b, p, i: (b, 0, p, 0)),
            pl.BlockSpec((1, nb, gw), lambda b, p, i: (b, 0, p)),
            _const_spec(bias.shape),
        ],
        out_specs=pl.BlockSpec((1, Q_BLOCK, gw), lambda b, p, i: (b, i, p)),
        out_shape=jax.ShapeDtypeStruct((B, S, width), _BF16),
        scratch_shapes=_attn_scratch(nb, groups, HEAD_DIM + BF16_ROWS),
        compiler_params=pltpu.CompilerParams(
            dimension_semantics=("parallel", "parallel", "arbitrary"),
            vmem_limit_bytes=VMEM_LIMIT_BYTES),
        name="moba_attention",
    )(qk, qk, vt, kmean, bias)


def _diff_body(q_ref, k_ref, vt_ref, bias_ref, lq1_ref, lk1_ref, lq2_ref, lk2_ref, subln_ref, o_ref,
               s0_ref, s1_ref, p_ref, a_ref, m_ref, acc_ref, *, lambda_init, groups):
    step = pl.program_id(2)
    qcats = [_stacked_queries(q_ref, g) for g in range(groups)]
    own_blk = 2 * step + _query_sub_block(groups * GROUP_COLS)

    def active_fn(j):
        return j <= own_blk

    _flash_blocks(step, qcats, k_ref, vt_ref, bias_ref, (s0_ref, s1_ref), p_ref, a_ref, m_ref,
                  acc_ref, active_fn, split_heads=False)

    lam = (jnp.exp(jnp.sum(lq1_ref[...] * lk1_ref[...], axis=-1, keepdims=True))
           - jnp.exp(jnp.sum(lq2_ref[...] * lk2_ref[...], axis=-1, keepdims=True))
           + lambda_init)
    for g in range(groups):
        acc = acc_ref[g]
        o = acc[:LANES] / acc[LANES:LANES + 1]
        a_t = o[:, :Q_BLOCK] - lam * o[:, Q_BLOCK:]
        a = a_t.T
        ms = jnp.mean(a * a, axis=-1, keepdims=True)
        out = (a * lax.rsqrt(ms + RMS_EPS) * subln_ref[...]) * (1.0 - lambda_init)
        o_ref[0, :, g * LANES:(g + 1) * LANES] = out.astype(o_ref.dtype)


def _diff_attention(qk, vt, bias, lq1, lk1, lq2, lk2, subln, *, width, lambda_init, groups):
    B, S, _ = qk.shape
    nb = S // KEY_BLOCK
    gw = groups * LANES
    chunks = width // gw
    body = functools.partial(_diff_body, lambda_init=lambda_init, groups=groups)
    vec = lambda n: _const_spec((1, n))
    return pl.pallas_call(
        body,
        grid=(B, chunks, S // Q_BLOCK),
        in_specs=[
            pl.BlockSpec((1, Q_BLOCK, gw), lambda b, h, i: (b, i, 2 * chunks + h)),
            pl.BlockSpec((1, S, gw), lambda b, h, i: (b, 0, 3 * chunks + h)),
            pl.BlockSpec((1, nb, gw, KEY_BLOCK), lambda b, h, i: (b, 0, chunks + h, 0)),
            _const_spec(bias.shape),
            vec(HEAD_DIM), vec(HEAD_DIM), vec(HEAD_DIM), vec(HEAD_DIM), vec(LANES),
        ],
        out_specs=pl.BlockSpec((1, Q_BLOCK, gw), lambda b, h, i: (b, i, h)),
        out_shape=jax.ShapeDtypeStruct((B, S, width), _BF16),
        scratch_shapes=_attn_scratch(0, groups, LANES + BF16_ROWS),
        compiler_params=pltpu.CompilerParams(
            dimension_semantics=("parallel", "parallel", "arbitrary"),
            vmem_limit_bytes=VMEM_LIMIT_BYTES),
        name="diff_attention",
    )(qk, qk, vt, bias, lq1, lk1, lq2, lk2, subln)


def _out_ffn_body(x_ref, mo_ref, do_ref, wo_ref, gain_ref, wg_ref, wu_ref, wd_ref, y_ref, *, width):
    x = x_ref[...]
    att = (jnp.dot(mo_ref[...], wo_ref[0:width, :], preferred_element_type=_F32)
           + jnp.dot(do_ref[...], wo_ref[width:2 * width, :], preferred_element_type=_F32))
    x1 = x + att
    ms = jnp.mean(x1 * x1, axis=-1, keepdims=True)
    h = (x1 * lax.rsqrt(ms + RMS_EPS) * gain_ref[...]).astype(_BF16)
    g = jnp.dot(h, wg_ref[...], preferred_element_type=_F32)
    u = jnp.dot(h, wu_ref[...], preferred_element_type=_F32)
    a = (g * jax.nn.sigmoid(g) * u).astype(_BF16)
    y_ref[...] = x1 + jnp.dot(a, wd_ref[...], preferred_element_type=_F32)


def _out_ffn(x2, mo2, do2, wo, gain, wg, wu, wd, *, tm):
    T, D = x2.shape
    width = mo2.shape[1]
    F = wg.shape[1]
    body = functools.partial(_out_ffn_body, width=width)
    row = lambda n: pl.BlockSpec((tm, n), lambda t: (t, 0))
    return pl.pallas_call(
        body,
        grid=(T // tm,),
        in_specs=[row(D), row(width), row(width), _const_spec((2 * width, D)), _const_spec((1, D)),
                  _const_spec((D, F)), _const_spec((D, F)), _const_spec((F, D))],
        out_specs=row(D),
        out_shape=jax.ShapeDtypeStruct((T, D), _F32),
        compiler_params=pltpu.CompilerParams(
            dimension_semantics=("parallel",), vmem_limit_bytes=VMEM_LIMIT_BYTES),
        name="outproj_swiglu",
    )(x2, mo2, do2, wo, gain, wg, wu, wd)


def _rope_tables(seq):
    inv = 1.0 / (ROPE_THETA ** (jnp.arange(0, HEAD_DIM, 2, dtype=_F32) / HEAD_DIM))
    ang = jnp.arange(seq, dtype=_F32)[:, None] * inv[None, :]
    cos = jnp.tile(jnp.cos(ang), (1, 2 * LANES // HEAD_DIM))
    sin = jnp.sin(ang)
    sin = jnp.tile(jnp.concatenate([-sin, sin], axis=-1), (1, LANES // HEAD_DIM))
    return cos, sin


def kernel(x, attn_norm, w_in, moba_q_norm, moba_k_norm, diff_q_norm, diff_k_norm,
           lambda_q1, lambda_k1, lambda_q2, lambda_k2, diff_subln, w_out,
           ffn_norm, w_gate, w_up, w_down):
    B, S, D = x.shape
    depth = w_in.shape[0]
    width = w_in.shape[2] // 6
    heads_per_group = width // HEAD_DIM
    tm_in = Q_BLOCK
    tm_ffn = Q_BLOCK
    cos, sin = _rope_tables(S)
    bias = _causal_bias()
    gmat = jnp.kron(jnp.eye(heads_per_group, dtype=_F32),
                    jnp.full((HEAD_DIM, HEAD_DIM), 1.0 / HEAD_DIM, _F32)).astype(_BF16)
    scale = HEAD_DIM ** -0.5 * math.log2(math.e)
    for l in range(depth):
        lambda_init = 0.8 - 0.6 * math.exp(-0.3 * l)
        tile = lambda g: jnp.tile(g.astype(_F32), heads_per_group)
        hgain = jnp.stack([tile(moba_q_norm[l]) * scale, tile(moba_k_norm[l]),
                           tile(diff_q_norm[l]) * scale, tile(diff_k_norm[l])])
        qk, vt, kmean = _inproj(x, attn_norm[l][None, :], w_in[l].astype(_BF16), gmat, hgain,
                                cos, sin, tm=tm_in)
        kmean = kmean.reshape(B, S // MOBA_BLOCK, width)
        moba_out = _moba_attention(qk, vt, kmean, bias, width=width, groups=ATTN_GROUPS)
        diff_out = _diff_attention(qk, vt, bias, lambda_q1[l][None, :], lambda_k1[l][None, :],
                                   lambda_q2[l][None, :], lambda_k2[l][None, :],
                                   diff_subln[l][None, :], width=width, lambda_init=lambda_init,
                                   groups=ATTN_GROUPS)
        y = _out_ffn(x.reshape(B * S, D), moba_out.reshape(B * S, width),
                     diff_out.reshape(B * S, width), w_out[l].astype(_BF16), ffn_norm[l][None, :],
                     w_gate[l].astype(_BF16), w_up[l].astype(_BF16), w_down[l].astype(_BF16),
                     tm=tm_ffn)
        x = y.reshape(B, S, D)
    return x
```

```python
import functools
import math

import jax
import jax.numpy as jnp
from jax import lax
from jax.experimental import pallas as pl
from jax.experimental.pallas import tpu as pltpu

HEAD_DIM = 64
LANES = 128
BF16_ROWS = 16
MOBA_BLOCK = 256
MOBA_TOPK = 3
ROPE_THETA = 10000.0
RMS_EPS = 1e-6
KEY_BLOCK = MOBA_BLOCK
Q_SUB = 2
Q_BLOCK = Q_SUB * MOBA_BLOCK
ATTN_GROUPS = 4
GROUP_COLS = 2 * Q_BLOCK
NEG_INIT = -1e30
VMEM_LIMIT_BYTES = 56 * 1024 * 1024

_NT = (((1,), (1,)), ((), ()))
_F32 = jnp.float32
_BF16 = jnp.bfloat16


def _const_spec(shape):
    zeros = (0,) * len(shape)
    return pl.BlockSpec(shape, lambda *_: zeros, pipeline_mode=pl.Buffered(1))


def _inproj_body(x_ref, gain_ref, w_ref, gmat_ref, hgain_ref, cos_ref, sin_ref,
                 qk_ref, vt_ref, kmean_ref, *, qk_cols, v_cols, width):
    x = x_ref[0]
    ms = jnp.mean(x * x, axis=-1, keepdims=True)
    h = (x * lax.rsqrt(ms + RMS_EPS) * gain_ref[...]).astype(_BF16)
    proj = jnp.dot(h, w_ref[...], preferred_element_type=_F32)

    tm = x.shape[0]
    cos = cos_ref[...]
    sin = sin_ref[...]

    for gi, c0 in enumerate(qk_cols):
        p = proj[:, c0:c0 + width]
        msq = jnp.dot((p * p).astype(_BF16), gmat_ref[...], preferred_element_type=_F32)
        y = p * lax.rsqrt(msq + RMS_EPS) * hgain_ref[gi:gi + 1, :]
        for cc in range(width // LANES):
            yc = y[:, cc * LANES:(cc + 1) * LANES]
            oc = yc * cos + pltpu.roll(yc, LANES // 2, 1) * sin
            qk_ref[0, :, gi * width + cc * LANES:gi * width + (cc + 1) * LANES] = oc.astype(_BF16)
            if gi == 1:
                for rb in range(tm // MOBA_BLOCK):
                    kmean_ref[0, 0, rb:rb + 1, cc * LANES:(cc + 1) * LANES] = jnp.mean(
                        oc[rb * MOBA_BLOCK:(rb + 1) * MOBA_BLOCK], axis=0, keepdims=True)

    for vi, c0 in enumerate(v_cols):
        for rb in range(tm // KEY_BLOCK):
            v = proj[rb * KEY_BLOCK:(rb + 1) * KEY_BLOCK, c0:c0 + width]
            vt_ref[0, rb, vi * width:(vi + 1) * width, :] = v.T.astype(_BF16)


def _inproj(x, gain, w, gmat, hgain, cos, sin, *, tm):
    B, S, D = x.shape
    n_out = w.shape[1]
    width = n_out // 6
    qk_cols = (0, width, 3 * width, 4 * width)
    v_cols = (2 * width, 5 * width)
    body = functools.partial(_inproj_body, qk_cols=qk_cols, v_cols=v_cols, width=width)
    return pl.pallas_call(
        body,
        grid=(B, S // tm),
        in_specs=[
            pl.BlockSpec((1, tm, D), lambda b, t: (b, t, 0)),
            _const_spec((1, D)),
            _const_spec((D, n_out)),
            _const_spec((width, width)),
            _const_spec((4, width)),
            pl.BlockSpec((tm, LANES), lambda b, t: (t, 0)),
            pl.BlockSpec((tm, LANES), lambda b, t: (t, 0)),
        ],
        out_specs=[
            pl.BlockSpec((1, tm, 4 * width), lambda b, t: (b, t, 0)),
            pl.BlockSpec((1, tm // KEY_BLOCK, 2 * width, KEY_BLOCK), lambda b, t: (b, t, 0, 0)),
            pl.BlockSpec((1, 1, tm // MOBA_BLOCK, width), lambda b, t: (b, t, 0, 0)),
        ],
        out_shape=[
            jax.ShapeDtypeStruct((B, S, 4 * width), _BF16),
            jax.ShapeDtypeStruct((B, S // KEY_BLOCK, 2 * width, KEY_BLOCK), _BF16),
            jax.ShapeDtypeStruct((B, S // tm, tm // MOBA_BLOCK, width), _F32),
        ],
        compiler_params=pltpu.CompilerParams(
            dimension_semantics=("parallel", "parallel"), vmem_limit_bytes=VMEM_LIMIT_BYTES),
        name="inproj_qknorm_rope",
    )(x, gain, w, gmat, hgain, cos, sin)


def _stacked_queries(q_ref, g):
    q = q_ref[0, :, g * LANES:(g + 1) * LANES]
    lane = lax.broadcasted_iota(jnp.int32, q.shape, 1)
    zero = jnp.zeros_like(q)
    first = (lane // (HEAD_DIM // 2)) % 2 == 0
    return jnp.concatenate([jnp.where(first, q, zero), jnp.where(first, zero, q)], axis=0)


def _query_sub_block(cols):
    col = lax.broadcasted_iota(jnp.int32, (1, cols), 1)
    return (col // MOBA_BLOCK) % Q_SUB


def _causal_bias():
    kpos = lax.broadcasted_iota(jnp.int32, (KEY_BLOCK, GROUP_COLS), 0)
    col = lax.broadcasted_iota(jnp.int32, (KEY_BLOCK, GROUP_COLS), 1)
    causal = kpos <= (col % MOBA_BLOCK)
    second = ((col // MOBA_BLOCK) % Q_SUB) == 1
    neg = jnp.float32(-jnp.inf)
    tiles = [jnp.where(causal | second, 0.0, neg), jnp.where(causal, 0.0, neg)]
    return jnp.stack(tiles).astype(_BF16)


def _flash_blocks(step, qcats, k_ref, vt_ref, bias_ref, s_refs, p_ref, a_ref, m_ref, acc_ref,
                  active_fn, all_past_active, split_heads):
    groups = len(qcats)
    n_chunks = KEY_BLOCK // BF16_ROWS
    ones = jnp.ones((BF16_ROWS, KEY_BLOCK), _BF16)
    own = 2 * step

    def scores(j, slot):
        kb = k_ref[0, pl.ds(pl.multiple_of(j * KEY_BLOCK, KEY_BLOCK), KEY_BLOCK), :]
        for g in range(groups):
            s_refs[slot][:, g * GROUP_COLS:(g + 1) * GROUP_COLS] = lax.dot_general(
                kb[:, g * LANES:(g + 1) * LANES], qcats[g], _NT,
                preferred_element_type=_F32).astype(_BF16)

    def softmax(j, slot, bias_tile=None):
        s_ref = s_refs[slot]
        cols = s_ref.shape[1]

        def rows(c):
            r = slice(c * BF16_ROWS, (c + 1) * BF16_ROWS)
            if bias_tile is None:
                return s_ref[r, :]
            bias = bias_ref[bias_tile, r, :]
            return jnp.concatenate([s_ref[r, g * GROUP_COLS:(g + 1) * GROUP_COLS] + bias
                                    for g in range(groups)], axis=1)

        mx = rows(0)
        for c in range(1, n_chunks):
            mx = jnp.maximum(mx, rows(c))
        bmax = jnp.max(mx.astype(_F32), axis=0, keepdims=True)
        m_old = m_ref[...]
        if bias_tile is None and all_past_active:
            m_new = jnp.maximum(m_old, bmax)
            m_used = m_new
        else:
            active = active_fn(j)
            m_new = jnp.where(active, jnp.maximum(m_old, bmax), m_old)
            m_used = jnp.where(active, m_new, jnp.inf)
        m_ref[...] = m_new
        a_ref[slot] = jnp.exp2(m_old - m_new)
        m_tile = jnp.broadcast_to(m_used, (BF16_ROWS, cols)).astype(_BF16)
        for c in range(n_chunks):
            p_ref[slot, c * BF16_ROWS:(c + 1) * BF16_ROWS, :] = jnp.exp2(rows(c) - m_tile)

    def values(j, slot):
        alpha = a_ref[slot]
        for g in range(groups):
            c0 = g * GROUP_COLS
            vt = vt_ref[0, j, g * LANES:(g + 1) * LANES, :]
            if split_heads:
                for h in range(2):
                    cols = slice(c0 + h * Q_BLOCK, c0 + (h + 1) * Q_BLOCK)
                    lhs = jnp.concatenate([vt[h * HEAD_DIM:(h + 1) * HEAD_DIM], ones], axis=0)
                    acc_ref[g, :, h * Q_BLOCK:(h + 1) * Q_BLOCK] = (
                        alpha[:, cols] * acc_ref[g, :, h * Q_BLOCK:(h + 1) * Q_BLOCK]
                        + jnp.dot(lhs, p_ref[slot, :, cols], preferred_element_type=_F32))
            else:
                cols = slice(c0, c0 + GROUP_COLS)
                lhs = jnp.concatenate([vt, ones], axis=0)
                acc_ref[g] = alpha[:, cols] * acc_ref[g] + jnp.dot(
                    lhs, p_ref[slot, :, cols], preferred_element_type=_F32)

    m_ref[...] = jnp.full(m_ref.shape, NEG_INIT, _F32)
    acc_ref[...] = jnp.zeros(acc_ref.shape, _F32)
    a_ref[1] = jnp.ones(a_ref.shape[1:], _F32)
    p_ref[1] = jnp.zeros(p_ref.shape[1:], _BF16)
    scores(0, 0)

    def past_pair(t, carry):
        j = 2 * t
        scores(j + 1, 1)
        values(jnp.maximum(j - 1, 0), 1)
        softmax(j, 0)
        scores(j + 2, 0)
        values(j, 0)
        softmax(j + 1, 1)
        return carry

    lax.fori_loop(0, step, past_pair, 0)
    scores(own + 1, 1)
    values(jnp.maximum(own - 1, 0), 1)
    softmax(own, 0, bias_tile=0)
    values(own, 0)
    softmax(own + 1, 1, bias_tile=1)
    values(own + 1, 1)


def _attn_scratch(nb_sel, groups, acc_rows):
    cols = groups * GROUP_COLS
    shapes = [pltpu.VMEM((KEY_BLOCK, cols), _BF16), pltpu.VMEM((KEY_BLOCK, cols), _BF16),
              pltpu.VMEM((2, KEY_BLOCK, cols), _BF16),
              pltpu.VMEM((2, 1, cols), _F32)]
    if nb_sel:
        shapes.append(pltpu.VMEM((nb_sel, cols), _F32))
    shapes += [pltpu.VMEM((1, cols), _F32),
               pltpu.VMEM((groups, acc_rows, GROUP_COLS), _F32)]
    return shapes


def _moba_body(q_ref, k_ref, vt_ref, kmean_ref, bias_ref, o_ref, s0_ref, s1_ref, p_ref, a_ref,
               sel_ref, m_ref, acc_ref, *, nb, groups):
    step = pl.program_id(2)
    blk = lax.broadcasted_iota(jnp.int32, (nb, GROUP_COLS), 0)
    own_blk = 2 * step + _query_sub_block(GROUP_COLS)
    valid = blk < own_blk
    blk_f = blk.astype(_F32)
    qcats = []
    for g in range(groups):
        qcat = _stacked_queries(q_ref, g)
        qcats.append(qcat)
        kmean = kmean_ref[0, :, g * LANES:(g + 1) * LANES].astype(_BF16)
        gate = lax.dot_general(kmean, qcat, _NT, preferred_element_type=_F32)
        gate = jnp.where(valid, gate, -jnp.inf)
        sel = jnp.where(blk == own_blk, 1.0, 0.0)
        for _ in range(MOBA_TOPK):
            mx = jnp.max(gate, axis=0, keepdims=True)
            idx = jnp.min(jnp.where(gate == mx, blk_f, float(nb)), axis=0, keepdims=True)
            pick = (blk_f == idx) & valid
            sel = jnp.where(pick, 1.0, sel)
            gate = jnp.where(pick, -jnp.inf, gate)
        sel_ref[:, g * GROUP_COLS:(g + 1) * GROUP_COLS] = sel

    def active_fn(j):
        return sel_ref[pl.ds(j, 1), :] > 0.0

    _flash_blocks(step, qcats, k_ref, vt_ref, bias_ref, (s0_ref, s1_ref), p_ref, a_ref, m_ref,
                  acc_ref, active_fn, all_past_active=False, split_heads=True)

    for g in range(groups):
        acc = acc_ref[g]
        o = acc[:HEAD_DIM] / acc[HEAD_DIM:HEAD_DIM + 1]
        o_t = jnp.concatenate([o[:, :Q_BLOCK], o[:, Q_BLOCK:]], axis=0)
        o_ref[0, :, g * LANES:(g + 1) * LANES] = o_t.T.astype(o_ref.dtype)


def _moba_attention(qk, vt, kmean, bias, *, width, groups):
    B, S, _ = qk.shape
    nb = S // KEY_BLOCK
    gw = groups * LANES
    chunks = width // gw
    body = functools.partial(_moba_body, nb=nb, groups=groups)
    return pl.pallas_call(
        body,
        grid=(B, chunks, S // Q_BLOCK),
        in_specs=[
            pl.BlockSpec((1, Q_BLOCK, gw), lambda b, p, i: (b, i, p)),
            pl.BlockSpec((1, S, gw), lambda b, p, i: (b, 0, chunks + p)),
            pl.BlockSpec((1, nb, gw, KEY_BLOCK), lambda b, p, i: (b, 0, p, 0)),
            pl.BlockSpec((1, nb, gw), lambda b, p, i: (b, 0, p)),
            _const_spec(bias.shape),
        ],
        out_specs=pl.BlockSpec((1, Q_BLOCK, gw), lambda b, p, i: (b, i, p)),
        out_shape=jax.ShapeDtypeStruct((B, S, width), _BF16),
        scratch_shapes=_attn_scratch(nb, groups, HEAD_DIM + BF16_ROWS),
        compiler_params=pltpu.CompilerParams(
            dimension_semantics=("parallel", "parallel", "arbitrary"),
            vmem_limit_bytes=VMEM_LIMIT_BYTES),
        name="moba_attention",
    )(qk, qk, vt, kmean, bias)


def _diff_body(q_ref, k_ref, vt_ref, bias_ref, lq1_ref, lk1_ref, lq2_ref, lk2_ref, subln_ref, o_ref,
               s0_ref, s1_ref, p_ref, a_ref, m_ref, acc_ref, *, lambda_init, groups):
    step = pl.program_id(2)
    qcats = [_stacked_queries(q_ref, g) for g in range(groups)]
    own_blk = 2 * step + _query_sub_block(groups * GROUP_COLS)

    def active_fn(j):
        return j <= own_blk

    _flash_blocks(step, qcats, k_ref, vt_ref, bias_ref, (s0_ref, s1_ref), p_ref, a_ref, m_ref,
                  acc_ref, active_fn, all_past_active=True, split_heads=False)

    lam = (jnp.exp(jnp.sum(lq1_ref[...] * lk1_ref[...], axis=-1, keepdims=True))
           - jnp.exp(jnp.sum(lq2_ref[...] * lk2_ref[...], axis=-1, keepdims=True))
           + lambda_init)
    for g in range(groups):
        acc = acc_ref[g]
        o = acc[:LANES] / acc[LANES:LANES + 1]
        a_t = o[:, :Q_BLOCK] - lam * o[:, Q_BLOCK:]
        a = a_t.T
        ms = jnp.mean(a * a, axis=-1, keepdims=True)
        out = (a * lax.rsqrt(ms + RMS_EPS) * subln_ref[...]) * (1.0 - lambda_init)
        o_ref[0, :, g * LANES:(g + 1) * LANES] = out.astype(o_ref.dtype)


def _diff_attention(qk, vt, bias, lq1, lk1, lq2, lk2, subln, *, width, lambda_init, groups):
    B, S, _ = qk.shape
    nb = S // KEY_BLOCK
    gw = groups * LANES
    chunks = width // gw
    body = functools.partial(_diff_body, lambda_init=lambda_init, groups=groups)
    vec = lambda n: _const_spec((1, n))
    return pl.pallas_call(
        body,
        grid=(B, chunks, S // Q_BLOCK),
        in_specs=[
            pl.BlockSpec((1, Q_BLOCK, gw), lambda b, h, i: (b, i, 2 * chunks + h)),
            pl.BlockSpec((1, S, gw), lambda b, h, i: (b, 0, 3 * chunks + h)),
            pl.BlockSpec((1, nb, gw, KEY_BLOCK), lambda b, h, i: (b, 0, chunks + h, 0)),
            _const_spec(bias.shape),
            vec(HEAD_DIM), vec(HEAD_DIM), vec(HEAD_DIM), vec(HEAD_DIM), vec(LANES),
        ],
        out_specs=pl.BlockSpec((1, Q_BLOCK, gw), lambda b, h, i: (b, i, h)),
        out_shape=jax.ShapeDtypeStruct((B, S, width), _BF16),
        scratch_shapes=_attn_scratch(0, groups, LANES + BF16_ROWS),
        compiler_params=pltpu.CompilerParams(
            dimension_semantics=("parallel", "parallel", "arbitrary"),
            vmem_limit_bytes=VMEM_LIMIT_BYTES),
        name="diff_attention",
    )(qk, qk, vt, bias, lq1, lk1, lq2, lk2, subln)


def _out_ffn_body(x_ref, mo_ref, do_ref, wo_ref, gain_ref, wg_ref, wu_ref, wd_ref, y_ref, *, width):
    x = x_ref[...]
    att = (jnp.dot(mo_ref[...], wo_ref[0:width, :], preferred_element_type=_F32)
           + jnp.dot(do_ref[...], wo_ref[width:2 * width, :], preferred_element_type=_F32))
    x1 = x + att
    ms = jnp.mean(x1 * x1, axis=-1, keepdims=True)
    h = (x1 * lax.rsqrt(ms + RMS_EPS) * gain_ref[...]).astype(_BF16)
    g = jnp.dot(h, wg_ref[...], preferred_element_type=_F32)
    u = jnp.dot(h, wu_ref[...], preferred_element_type=_F32)
    a = (g * jax.nn.sigmoid(g) * u).astype(_BF16)
    y_ref[...] = x1 + jnp.dot(a, wd_ref[...], preferred_element_type=_F32)


def _out_ffn(x2, mo2, do2, wo, gain, wg, wu, wd, *, tm):
    T, D = x2.shape
    width = mo2.shape[1]
    F = wg.shape[1]
    body = functools.partial(_out_ffn_body, width=width)
    row = lambda n: pl.BlockSpec((tm, n), lambda t: (t, 0))
    return pl.pallas_call(
        body,
        grid=(T // tm,),
        in_specs=[row(D), row(width), row(width), _const_spec((2 * width, D)), _const_spec((1, D)),
                  _const_spec((D, F)), _const_spec((D, F)), _const_spec((F, D))],
        out_specs=row(D),
        out_shape=jax.ShapeDtypeStruct((T, D), _F32),
        compiler_params=pltpu.CompilerParams(
            dimension_semantics=("parallel",), vmem_limit_bytes=VMEM_LIMIT_BYTES),
        name="outproj_swiglu",
    )(x2, mo2, do2, wo, gain, wg, wu, wd)


def _qk_lane_order(a):
    half = HEAD_DIM // 2
    lead = a.shape[:-1]
    a = a.reshape(*lead, a.shape[-1] // LANES, 2, 2, half)
    return jnp.swapaxes(a, -3, -2).reshape(*lead, -1)


def _rope_tables(seq):
    inv = 1.0 / (ROPE_THETA ** (jnp.arange(0, HEAD_DIM, 2, dtype=_F32) / HEAD_DIM))
    ang = jnp.arange(seq, dtype=_F32)[:, None] * inv[None, :]
    cos = jnp.tile(jnp.cos(ang), (1, 2 * LANES // HEAD_DIM))
    sin = jnp.sin(ang)
    sin = jnp.concatenate([-sin, -sin, sin, sin], axis=-1)
    return cos, sin


def kernel(x, attn_norm, w_in, moba_q_norm, moba_k_norm, diff_q_norm, diff_k_norm,
           lambda_q1, lambda_k1, lambda_q2, lambda_k2, diff_subln, w_out,
           ffn_norm, w_gate, w_up, w_down):
    B, S, D = x.shape
    depth = w_in.shape[0]
    width = w_in.shape[2] // 6
    heads_per_group = width // HEAD_DIM
    tm_in = Q_BLOCK
    tm_ffn = Q_BLOCK
    cos, sin = _rope_tables(S)
    bias = _causal_bias()
    gmat = jnp.kron(jnp.eye(heads_per_group, dtype=_F32),
                    jnp.full((HEAD_DIM, HEAD_DIM), 1.0 / HEAD_DIM, _F32))
    gmat = _qk_lane_order(_qk_lane_order(gmat).T).astype(_BF16)
    scale = HEAD_DIM ** -0.5 * math.log2(math.e)
    for l in range(depth):
        lambda_init = 0.8 - 0.6 * math.exp(-0.3 * l)
        tile = lambda g: _qk_lane_order(jnp.tile(g.astype(_F32), heads_per_group))
        hgain = jnp.stack([tile(moba_q_norm[l]) * scale, tile(moba_k_norm[l]),
                           tile(diff_q_norm[l]) * scale, tile(diff_k_norm[l])])
        w = w_in[l].astype(_BF16).reshape(D, 6, width)
        w = jnp.concatenate([_qk_lane_order(w[:, :2]), w[:, 2:3], _qk_lane_order(w[:, 3:5]),
                             w[:, 5:]], axis=1).reshape(D, 6 * width)
        qk, vt, kmean = _inproj(x, attn_norm[l][None, :], w, gmat, hgain, cos, sin, tm=tm_in)
        kmean = kmean.reshape(B, S // MOBA_BLOCK, width)
        moba_out = _moba_attention(qk, vt, kmean, bias, width=width, groups=ATTN_GROUPS)
        diff_out = _diff_attention(qk, vt, bias, lambda_q1[l][None, :], lambda_k1[l][None, :],
                                   lambda_q2[l][None, :], lambda_k2[l][None, :],
                                   diff_subln[l][None, :], width=width, lambda_init=lambda_init,
                                   groups=ATTN_GROUPS)
        y = _out_ffn(x.reshape(B * S, D), moba_out.reshape(B * S, width),
                     diff_out.reshape(B * S, width), w_out[l].astype(_BF16), ffn_norm[l][None, :],
                     w_gate[l].astype(_BF16), w_up[l].astype(_BF16), w_down[l].astype(_BF16),
                     tm=tm_ffn)
        x = y.reshape(B, S, D)
    return x
```

```python
import functools
import math

import jax
import jax.numpy as jnp
from jax import lax
from jax.experimental import pallas as pl
from jax.experimental.pallas import tpu as pltpu

HEAD_DIM = 64
LANES = 128
BF16_ROWS = 16
MOBA_BLOCK = 256
MOBA_TOPK = 3
ROPE_THETA = 10000.0
RMS_EPS = 1e-6
KEY_BLOCK = MOBA_BLOCK
Q_SUB = 2
Q_BLOCK = Q_SUB * MOBA_BLOCK
ATTN_GROUPS = 4
GROUP_COLS = 2 * Q_BLOCK
NEG_INIT = -1e30
VMEM_LIMIT_BYTES = 56 * 1024 * 1024

_NT = (((1,), (1,)), ((), ()))
_F32 = jnp.float32
_BF16 = jnp.bfloat16


def _const_spec(shape):
    zeros = (0,) * len(shape)
    return pl.BlockSpec(shape, lambda *_: zeros, pipeline_mode=pl.Buffered(1))


def _inproj_body(x_ref, gain_ref, w_ref, gmat_ref, hgain_ref, cos_ref, sin_ref,
                 qk_ref, vt_ref, kmean_ref, *, qk_cols, v_cols, width):
    x = x_ref[0]
    ms = jnp.mean(x * x, axis=-1, keepdims=True)
    h = (x * lax.rsqrt(ms + RMS_EPS) * gain_ref[...]).astype(_BF16)
    proj = jnp.dot(h, w_ref[...], preferred_element_type=_F32)

    tm = x.shape[0]
    cos = cos_ref[...]
    sin = sin_ref[...]

    for gi, c0 in enumerate(qk_cols):
        p = proj[:, c0:c0 + width]
        msq = jnp.dot((p * p).astype(_BF16), gmat_ref[...], preferred_element_type=_F32)
        y = p * lax.rsqrt(msq + RMS_EPS) * hgain_ref[gi:gi + 1, :]
        for cc in range(width // LANES):
            yc = y[:, cc * LANES:(cc + 1) * LANES]
            oc = yc * cos + pltpu.roll(yc, LANES // 2, 1) * sin
            qk_ref[0, :, gi * width + cc * LANES:gi * width + (cc + 1) * LANES] = oc.astype(_BF16)
            if gi == 1:
                for rb in range(tm // MOBA_BLOCK):
                    kmean_ref[0, 0, rb:rb + 1, cc * LANES:(cc + 1) * LANES] = jnp.mean(
                        oc[rb * MOBA_BLOCK:(rb + 1) * MOBA_BLOCK], axis=0, keepdims=True)

    for vi, c0 in enumerate(v_cols):
        for rb in range(tm // KEY_BLOCK):
            v = proj[rb * KEY_BLOCK:(rb + 1) * KEY_BLOCK, c0:c0 + width]
            vt_ref[0, rb, vi * width:(vi + 1) * width, :] = v.T.astype(_BF16)


def _inproj(x, gain, w, gmat, hgain, cos, sin, *, tm):
    B, S, D = x.shape
    n_out = w.shape[1]
    width = n_out // 6
    qk_cols = (0, width, 3 * width, 4 * width)
    v_cols = (2 * width, 5 * width)
    body = functools.partial(_inproj_body, qk_cols=qk_cols, v_cols=v_cols, width=width)
    return pl.pallas_call(
        body,
        grid=(B, S // tm),
        in_specs=[
            pl.BlockSpec((1, tm, D), lambda b, t: (b, t, 0)),
            _const_spec((1, D)),
            _const_spec((D, n_out)),
            _const_spec((width, width)),
            _const_spec((4, width)),
            pl.BlockSpec((tm, LANES), lambda b, t: (t, 0)),
            pl.BlockSpec((tm, LANES), lambda b, t: (t, 0)),
        ],
        out_specs=[
            pl.BlockSpec((1, tm, 4 * width), lambda b, t: (b, t, 0)),
            pl.BlockSpec((1, tm // KEY_BLOCK, 2 * width, KEY_BLOCK), lambda b, t: (b, t, 0, 0)),
            pl.BlockSpec((1, 1, tm // MOBA_BLOCK, width), lambda b, t: (b, t, 0, 0)),
        ],
        out_shape=[
            jax.ShapeDtypeStruct((B, S, 4 * width), _BF16),
            jax.ShapeDtypeStruct((B, S // KEY_BLOCK, 2 * width, KEY_BLOCK), _BF16),
            jax.ShapeDtypeStruct((B, S // tm, tm // MOBA_BLOCK, width), _F32),
        ],
        compiler_params=pltpu.CompilerParams(
            dimension_semantics=("parallel", "parallel"), vmem_limit_bytes=VMEM_LIMIT_BYTES),
        name="inproj_qknorm_rope",
    )(x, gain, w, gmat, hgain, cos, sin)


def _stacked_queries(q_ref, g):
    q = q_ref[0, :, g * LANES:(g + 1) * LANES]
    lane = lax.broadcasted_iota(jnp.int32, q.shape, 1)
    zero = jnp.zeros_like(q)
    first = (lane // (HEAD_DIM // 2)) % 2 == 0
    stacked = jnp.concatenate([jnp.where(first, q, zero), jnp.where(first, zero, q)], axis=0)
    return stacked.astype(_F32).T.astype(_BF16)


def _query_sub_block(cols):
    col = lax.broadcasted_iota(jnp.int32, (1, cols), 1)
    return (col // MOBA_BLOCK) % Q_SUB


def _causal_bias():
    kpos = lax.broadcasted_iota(jnp.int32, (KEY_BLOCK, GROUP_COLS), 0)
    col = lax.broadcasted_iota(jnp.int32, (KEY_BLOCK, GROUP_COLS), 1)
    causal = kpos <= (col % MOBA_BLOCK)
    second = ((col // MOBA_BLOCK) % Q_SUB) == 1
    neg = jnp.float32(-jnp.inf)
    tiles = [jnp.where(causal | second, 0.0, neg), jnp.where(causal, 0.0, neg)]
    return jnp.stack(tiles).astype(_BF16)


def _flash_blocks(step, qcats, k_ref, vt_ref, bias_ref, s_refs, p_ref, a_ref, m_ref, acc_ref,
                  active_fn, all_past_active, split_heads):
    groups = len(qcats)
    n_chunks = KEY_BLOCK // BF16_ROWS
    ones = jnp.ones((BF16_ROWS, KEY_BLOCK), _BF16)
    own = 2 * step

    def scores(j, slot):
        kb = k_ref[0, pl.ds(pl.multiple_of(j * KEY_BLOCK, KEY_BLOCK), KEY_BLOCK), :]
        for g in range(groups):
            s_refs[slot][:, g * GROUP_COLS:(g + 1) * GROUP_COLS] = jnp.dot(
                kb[:, g * LANES:(g + 1) * LANES], qcats[g],
                preferred_element_type=_F32).astype(_BF16)

    def softmax(j, slot, bias_tile=None):
        s_ref = s_refs[slot]
        cols = s_ref.shape[1]

        def rows(c):
            r = slice(c * BF16_ROWS, (c + 1) * BF16_ROWS)
            if bias_tile is None:
                return s_ref[r, :]
            bias = bias_ref[bias_tile, r, :]
            return jnp.concatenate([s_ref[r, g * GROUP_COLS:(g + 1) * GROUP_COLS] + bias
                                    for g in range(groups)], axis=1)

        mx = rows(0)
        for c in range(1, n_chunks):
            mx = jnp.maximum(mx, rows(c))
        bmax = jnp.max(mx.astype(_F32), axis=0, keepdims=True)
        m_old = m_ref[...]
        if bias_tile is None and all_past_active:
            m_new = jnp.maximum(m_old, bmax)
            m_used = m_new
        else:
            active = active_fn(j)
            m_new = jnp.where(active, jnp.maximum(m_old, bmax), m_old)
            m_used = jnp.where(active, m_new, jnp.inf)
        m_ref[...] = m_new
        a_ref[slot] = jnp.exp2(m_old - m_new)
        m_tile = jnp.broadcast_to(m_used, (BF16_ROWS, cols)).astype(_BF16)
        for c in range(n_chunks):
            p_ref[slot, c * BF16_ROWS:(c + 1) * BF16_ROWS, :] = jnp.exp2(rows(c) - m_tile)

    def values(j, slot):
        alpha = a_ref[slot]
        for g in range(groups):
            c0 = g * GROUP_COLS
            vt = vt_ref[0, j, g * LANES:(g + 1) * LANES, :]
            if split_heads:
                for h in range(2):
                    cols = slice(c0 + h * Q_BLOCK, c0 + (h + 1) * Q_BLOCK)
                    lhs = jnp.concatenate([vt[h * HEAD_DIM:(h + 1) * HEAD_DIM], ones], axis=0)
                    acc_ref[g, :, h * Q_BLOCK:(h + 1) * Q_BLOCK] = (
                        alpha[:, cols] * acc_ref[g, :, h * Q_BLOCK:(h + 1) * Q_BLOCK]
                        + jnp.dot(lhs, p_ref[slot, :, cols], preferred_element_type=_F32))
            else:
                cols = slice(c0, c0 + GROUP_COLS)
                lhs = jnp.concatenate([vt, ones], axis=0)
                acc_ref[g] = alpha[:, cols] * acc_ref[g] + jnp.dot(
                    lhs, p_ref[slot, :, cols], preferred_element_type=_F32)

    m_ref[...] = jnp.full(m_ref.shape, NEG_INIT, _F32)
    acc_ref[...] = jnp.zeros(acc_ref.shape, _F32)
    a_ref[1] = jnp.ones(a_ref.shape[1:], _F32)
    p_ref[1] = jnp.zeros(p_ref.shape[1:], _BF16)
    scores(0, 0)

    def past_pair(t, carry):
        j = 2 * t
        scores(j + 1, 1)
        values(jnp.maximum(j - 1, 0), 1)
        softmax(j, 0)
        scores(j + 2, 0)
        values(j, 0)
        softmax(j + 1, 1)
        return carry

    lax.fori_loop(0, step, past_pair, 0)
    scores(own + 1, 1)
    values(jnp.maximum(own - 1, 0), 1)
    softmax(own, 0, bias_tile=0)
    values(own, 0)
    softmax(own + 1, 1, bias_tile=1)
    values(own + 1, 1)


def _attn_scratch(nb_sel, groups, acc_rows):
    cols = groups * GROUP_COLS
    shapes = [pltpu.VMEM((KEY_BLOCK, cols), _BF16), pltpu.VMEM((KEY_BLOCK, cols), _BF16),
              pltpu.VMEM((2, KEY_BLOCK, cols), _BF16),
              pltpu.VMEM((2, 1, cols), _F32)]
    if nb_sel:
        shapes.append(pltpu.VMEM((nb_sel, cols), _F32))
    shapes += [pltpu.VMEM((1, cols), _F32),
               pltpu.VMEM((groups, acc_rows, GROUP_COLS), _F32)]
    return shapes


def _moba_body(q_ref, k_ref, vt_ref, kmean_ref, bias_ref, o_ref, s0_ref, s1_ref, p_ref, a_ref,
               sel_ref, m_ref, acc_ref, *, nb, groups):
    step = pl.program_id(2)
    blk = lax.broadcasted_iota(jnp.int32, (nb, GROUP_COLS), 0)
    own_blk = 2 * step + _query_sub_block(GROUP_COLS)
    valid = blk < own_blk
    blk_f = blk.astype(_F32)
    qcats = []
    for g in range(groups):
        qcat = _stacked_queries(q_ref, g)
        qcats.append(qcat)
        kmean = kmean_ref[0, :, g * LANES:(g + 1) * LANES].astype(_BF16)
        gate = jnp.dot(kmean, qcat, preferred_element_type=_F32)
        gate = jnp.where(valid, gate, -jnp.inf)
        sel = jnp.where(blk == own_blk, 1.0, 0.0)
        for _ in range(MOBA_TOPK):
            mx = jnp.max(gate, axis=0, keepdims=True)
            idx = jnp.min(jnp.where(gate == mx, blk_f, float(nb)), axis=0, keepdims=True)
            pick = (blk_f == idx) & valid
            sel = jnp.where(pick, 1.0, sel)
            gate = jnp.where(pick, -jnp.inf, gate)
        sel_ref[:, g * GROUP_COLS:(g + 1) * GROUP_COLS] = sel

    def active_fn(j):
        return sel_ref[pl.ds(j, 1), :] > 0.0

    _flash_blocks(step, qcats, k_ref, vt_ref, bias_ref, (s0_ref, s1_ref), p_ref, a_ref, m_ref,
                  acc_ref, active_fn, all_past_active=False, split_heads=True)

    for g in range(groups):
        acc = acc_ref[g]
        o = acc[:HEAD_DIM] / acc[HEAD_DIM:HEAD_DIM + 1]
        o_t = jnp.concatenate([o[:, :Q_BLOCK], o[:, Q_BLOCK:]], axis=0)
        o_ref[0, :, g * LANES:(g + 1) * LANES] = o_t.T.astype(o_ref.dtype)


def _moba_attention(qk, vt, kmean, bias, *, width, groups):
    B, S, _ = qk.shape
    nb = S // KEY_BLOCK
    gw = groups * LANES
    chunks = width // gw
    body = functools.partial(_moba_body, nb=nb, groups=groups)
    return pl.pallas_call(
        body,
        grid=(B, chunks, S // Q_BLOCK),
        in_specs=[
            pl.BlockSpec((1, Q_BLOCK, gw), lambda b, p, i: (b, i, p)),
            pl.BlockSpec((1, S, gw), lambda b, p, i: (b, 0, chunks + p)),
            pl.BlockSpec((1, nb, gw, KEY_BLOCK), lambda b, p, i: (b, 0, p, 0)),
            pl.BlockSpec((1, nb, gw), lambda b, p, i: (b, 0, p)),
            _const_spec(bias.shape),
        ],
        out_specs=pl.BlockSpec((1, Q_BLOCK, gw), lambda b, p, i: (b, i, p)),
        out_shape=jax.ShapeDtypeStruct((B, S, width), _BF16),
        scratch_shapes=_attn_scratch(nb, groups, HEAD_DIM + BF16_ROWS),
        compiler_params=pltpu.CompilerParams(
            dimension_semantics=("parallel", "parallel", "arbitrary"),
            vmem_limit_bytes=VMEM_LIMIT_BYTES),
        name="moba_attention",
    )(qk, qk, vt, kmean, bias)


def _diff_body(q_ref, k_ref, vt_ref, bias_ref, lq1_ref, lk1_ref, lq2_ref, lk2_ref, subln_ref, o_ref,
               s0_ref, s1_ref, p_ref, a_ref, m_ref, acc_ref, *, lambda_init, groups):
    step = pl.program_id(2)
    qcats = [_stacked_queries(q_ref, g) for g in range(groups)]
    own_blk = 2 * step + _query_sub_block(groups * GROUP_COLS)

    def active_fn(j):
        return j <= own_blk

    _flash_blocks(step, qcats, k_ref, vt_ref, bias_ref, (s0_ref, s1_ref), p_ref, a_ref, m_ref,
                  acc_ref, active_fn, all_past_active=True, split_heads=False)

    lam = (jnp.exp(jnp.sum(lq1_ref[...] * lk1_ref[...], axis=-1, keepdims=True))
           - jnp.exp(jnp.sum(lq2_ref[...] * lk2_ref[...], axis=-1, keepdims=True))
           + lambda_init)
    for g in range(groups):
        acc = acc_ref[g]
        o = acc[:LANES] / acc[LANES:LANES + 1]
        a_t = o[:, :Q_BLOCK] - lam * o[:, Q_BLOCK:]
        a = a_t.T
        ms = jnp.mean(a * a, axis=-1, keepdims=True)
        out = (a * lax.rsqrt(ms + RMS_EPS) * subln_ref[...]) * (1.0 - lambda_init)
        o_ref[0, :, g * LANES:(g + 1) * LANES] = out.astype(o_ref.dtype)


def _diff_attention(qk, vt, bias, lq1, lk1, lq2, lk2, subln, *, width, lambda_init, groups):
    B, S, _ = qk.shape
    nb = S // KEY_BLOCK
    gw = groups * LANES
    chunks = width // gw
    body = functools.partial(_diff_body, lambda_init=lambda_init, groups=groups)
    vec = lambda n: _const_spec((1, n))
    return pl.pallas_call(
        body,
        grid=(B, chunks, S // Q_BLOCK),
        in_specs=[
            pl.BlockSpec((1, Q_BLOCK, gw), lambda b, h, i: (b, i, 2 * chunks + h)),
            pl.BlockSpec((1, S, gw), lambda b, h, i: (b, 0, 3 * chunks + h)),
            pl.BlockSpec((1, nb, gw, KEY_BLOCK), lambda b, h, i: (b, 0, chunks + h, 0)),
            _const_spec(bias.shape),
            vec(HEAD_DIM), vec(HEAD_DIM), vec(HEAD_DIM), vec(HEAD_DIM), vec(LANES),
        ],
        out_specs=pl.BlockSpec((1, Q_BLOCK, gw), lambda b, h, i: (b, i, h)),
        out_shape=jax.ShapeDtypeStruct((B, S, width), _BF16),
        scratch_shapes=_attn_scratch(0, groups, LANES + BF16_ROWS),
        compiler_params=pltpu.CompilerParams(
            dimension_semantics=("parallel", "parallel", "arbitrary"),
            vmem_limit_bytes=VMEM_LIMIT_BYTES),
        name="diff_attention",
    )(qk, qk, vt, bias, lq1, lk1, lq2, lk2, subln)


def _out_ffn_body(x_ref, mo_ref, do_ref, wo_ref, gain_ref, wg_ref, wu_ref, wd_ref, y_ref, *, width):
    x = x_ref[...]
    att = (jnp.dot(mo_ref[...], wo_ref[0:width, :], preferred_element_type=_F32)
           + jnp.dot(do_ref[...], wo_ref[width:2 * width, :], preferred_element_type=_F32))
    x1 = x + att
    ms = jnp.mean(x1 * x1, axis=-1, keepdims=True)
    h = (x1 * lax.rsqrt(ms + RMS_EPS) * gain_ref[...]).astype(_BF16)
    g = jnp.dot(h, wg_ref[...], preferred_element_type=_F32)
    u = jnp.dot(h, wu_ref[...], preferred_element_type=_F32)
    a = (g * jax.nn.sigmoid(g) * u).astype(_BF16)
    y_ref[...] = x1 + jnp.dot(a, wd_ref[...], preferred_element_type=_F32)


def _out_ffn(x2, mo2, do2, wo, gain, wg, wu, wd, *, tm):
    T, D = x2.shape
    width = mo2.shape[1]
    F = wg.shape[1]
    body = functools.partial(_out_ffn_body, width=width)
    row = lambda n: pl.BlockSpec((tm, n), lambda t: (t, 0))
    return pl.pallas_call(
        body,
        grid=(T // tm,),
        in_specs=[row(D), row(width), row(width), _const_spec((2 * width, D)), _const_spec((1, D)),
                  _const_spec((D, F)), _const_spec((D, F)), _const_spec((F, D))],
        out_specs=row(D),
        out_shape=jax.ShapeDtypeStruct((T, D), _F32),
        compiler_params=pltpu.CompilerParams(
            dimension_semantics=("parallel",), vmem_limit_bytes=VMEM_LIMIT_BYTES),
        name="outproj_swiglu",
    )(x2, mo2, do2, wo, gain, wg, wu, wd)


def _qk_lane_order(a):
    half = HEAD_DIM // 2
    lead = a.shape[:-1]
    a = a.reshape(*lead, a.shape[-1] // LANES, 2, 2, half)
    return jnp.swapaxes(a, -3, -2).reshape(*lead, -1)


def _rope_tables(seq):
    inv = 1.0 / (ROPE_THETA ** (jnp.arange(0, HEAD_DIM, 2, dtype=_F32) / HEAD_DIM))
    ang = jnp.arange(seq, dtype=_F32)[:, None] * inv[None, :]
    cos = jnp.tile(jnp.cos(ang), (1, 2 * LANES // HEAD_DIM))
    sin = jnp.sin(ang)
    sin = jnp.concatenate([-sin, -sin, sin, sin], axis=-1)
    return cos, sin


def kernel(x, attn_norm, w_in, moba_q_norm, moba_k_norm, diff_q_norm, diff_k_norm,
           lambda_q1, lambda_k1, lambda_q2, lambda_k2, diff_subln, w_out,
           ffn_norm, w_gate, w_up, w_down):
    B, S, D = x.shape
    depth = w_in.shape[0]
    width = w_in.shape[2] // 6
    heads_per_group = width // HEAD_DIM
    tm_in = Q_BLOCK
    tm_ffn = Q_BLOCK
    cos, sin = _rope_tables(S)
    bias = _causal_bias()
    gmat = jnp.kron(jnp.eye(heads_per_group, dtype=_F32),
                    jnp.full((HEAD_DIM, HEAD_DIM), 1.0 / HEAD_DIM, _F32))
    gmat = _qk_lane_order(_qk_lane_order(gmat).T).astype(_BF16)
    scale = HEAD_DIM ** -0.5 * math.log2(math.e)
    for l in range(depth):
        lambda_init = 0.8 - 0.6 * math.exp(-0.3 * l)
        tile = lambda g: _qk_lane_order(jnp.tile(g.astype(_F32), heads_per_group))
        hgain = jnp.stack([tile(moba_q_norm[l]) * scale, tile(moba_k_norm[l]),
                           tile(diff_q_norm[l]) * scale, tile(diff_k_norm[l])])
        w = w_in[l].astype(_BF16).reshape(D, 6, width)
        w = jnp.concatenate([_qk_lane_order(w[:, :2]), w[:, 2:3], _qk_lane_order(w[:, 3:5]),
                             w[:, 5:]], axis=1).reshape(D, 6 * width)
        qk, vt, kmean = _inproj(x, attn_norm[l][None, :], w, gmat, hgain, cos, sin, tm=tm_in)
        kmean = kmean.reshape(B, S // MOBA_BLOCK, width)
        moba_out = _moba_attention(qk, vt, kmean, bias, width=width, groups=ATTN_GROUPS)
        diff_out = _diff_attention(qk, vt, bias, lambda_q1[l][None, :], lambda_k1[l][None, :],
                                   lambda_q2[l][None, :], lambda_k2[l][None, :],
                                   diff_subln[l][None, :], width=width, lambda_init=lambda_init,
                                   groups=ATTN_GROUPS)
        y = _out_ffn(x.reshape(B * S, D), moba_out.reshape(B * S, width),
                     diff_out.reshape(B * S, width), w_out[l].astype(_BF16), ffn_norm[l][None, :],
                     w_gate[l].astype(_BF16), w_up[l].astype(_BF16), w_down[l].astype(_BF16),
                     tm=tm_ffn)
        x = y.reshape(B, S, D)
    return x
```

```python
import functools
import math

import jax
import jax.numpy as jnp
from jax import lax
from jax.experimental import pallas as pl
from jax.experimental.pallas import tpu as pltpu

HEAD_DIM = 64
LANES = 128
BF16_ROWS = 16
MXU_COLS = 256
MOBA_BLOCK = 256
MOBA_TOPK = 3
ROPE_THETA = 10000.0
RMS_EPS = 1e-6
KEY_BLOCK = MOBA_BLOCK
Q_SUB = 2
Q_BLOCK = Q_SUB * MOBA_BLOCK
ATTN_GROUPS = 4
GROUP_COLS = 2 * Q_BLOCK
NEG_INIT = -1e30
VMEM_LIMIT_BYTES = 56 * 1024 * 1024

_F32 = jnp.float32
_BF16 = jnp.bfloat16


def _const_spec(shape):
    zeros = (0,) * len(shape)
    return pl.BlockSpec(shape, lambda *_: zeros, pipeline_mode=pl.Buffered(1))


def _inproj_body(x_ref, gain_ref, w_ref, gmat_ref, hgain_ref, cos_ref, sin_ref,
                 qk_ref, vt_ref, kmean_ref, *, qk_cols, v_cols, width):
    x = x_ref[0]
    ms = jnp.mean(x * x, axis=-1, keepdims=True)
    h = (x * lax.rsqrt(ms + RMS_EPS) * gain_ref[...]).astype(_BF16)

    def project(c0):
        return jnp.dot(h, w_ref[:, c0:c0 + width], preferred_element_type=_F32)

    tm = x.shape[0]
    cos = cos_ref[...]
    sin = sin_ref[...]

    for gi, c0 in enumerate(qk_cols):
        p = project(c0)
        sq = (p * p).astype(_BF16)
        msq = jnp.concatenate(
            [jnp.dot(sq[:, c:c + MXU_COLS], gmat_ref[...], preferred_element_type=_F32)
             for c in range(0, width, MXU_COLS)], axis=1)
        y = p * lax.rsqrt(msq + RMS_EPS) * hgain_ref[gi:gi + 1, :]
        for cc in range(width // LANES):
            yc = y[:, cc * LANES:(cc + 1) * LANES]
            oc = yc * cos + pltpu.roll(yc, LANES // 2, 1) * sin
            qk_ref[0, :, gi * width + cc * LANES:gi * width + (cc + 1) * LANES] = oc.astype(_BF16)
            if gi == 1:
                for rb in range(tm // MOBA_BLOCK):
                    kmean_ref[0, 0, rb:rb + 1, cc * LANES:(cc + 1) * LANES] = jnp.mean(
                        oc[rb * MOBA_BLOCK:(rb + 1) * MOBA_BLOCK], axis=0, keepdims=True)

    for vi, c0 in enumerate(v_cols):
        v = project(c0)
        for rb in range(tm // KEY_BLOCK):
            vt_ref[0, rb, vi * width:(vi + 1) * width, :] = (
                v[rb * KEY_BLOCK:(rb + 1) * KEY_BLOCK].T.astype(_BF16))


def _inproj(x, gain, w, gmat, hgain, cos, sin, *, tm):
    B, S, D = x.shape
    n_out = w.shape[1]
    width = n_out // 6
    qk_cols = (0, width, 3 * width, 4 * width)
    v_cols = (2 * width, 5 * width)
    body = functools.partial(_inproj_body, qk_cols=qk_cols, v_cols=v_cols, width=width)
    return pl.pallas_call(
        body,
        grid=(B, S // tm),
        in_specs=[
            pl.BlockSpec((1, tm, D), lambda b, t: (b, t, 0)),
            _const_spec((1, D)),
            _const_spec((D, n_out)),
            _const_spec((MXU_COLS, MXU_COLS)),
            _const_spec((4, width)),
            pl.BlockSpec((tm, LANES), lambda b, t: (t, 0)),
            pl.BlockSpec((tm, LANES), lambda b, t: (t, 0)),
        ],
        out_specs=[
            pl.BlockSpec((1, tm, 4 * width), lambda b, t: (b, t, 0)),
            pl.BlockSpec((1, tm // KEY_BLOCK, 2 * width, KEY_BLOCK), lambda b, t: (b, t, 0, 0)),
            pl.BlockSpec((1, 1, tm // MOBA_BLOCK, width), lambda b, t: (b, t, 0, 0)),
        ],
        out_shape=[
            jax.ShapeDtypeStruct((B, S, 4 * width), _BF16),
            jax.ShapeDtypeStruct((B, S // KEY_BLOCK, 2 * width, KEY_BLOCK), _BF16),
            jax.ShapeDtypeStruct((B, S // tm, tm // MOBA_BLOCK, width), _F32),
        ],
        compiler_params=pltpu.CompilerParams(
            dimension_semantics=("parallel", "parallel"), vmem_limit_bytes=VMEM_LIMIT_BYTES),
        name="inproj_qknorm_rope",
    )(x, gain, w, gmat, hgain, cos, sin)


def _stacked_queries(q_ref, g):
    q = q_ref[0, :, g * LANES:(g + 1) * LANES]
    lane = lax.broadcasted_iota(jnp.int32, q.shape, 1)
    zero = jnp.zeros_like(q)
    first = (lane // (HEAD_DIM // 2)) % 2 == 0
    stacked = jnp.concatenate([jnp.where(first, q, zero), jnp.where(first, zero, q)], axis=0)
    return stacked.astype(_F32).T.astype(_BF16)


def _query_sub_block(cols):
    col = lax.broadcasted_iota(jnp.int32, (1, cols), 1)
    return (col // MOBA_BLOCK) % Q_SUB


def _causal_bias():
    kpos = lax.broadcasted_iota(jnp.int32, (KEY_BLOCK, GROUP_COLS), 0)
    col = lax.broadcasted_iota(jnp.int32, (KEY_BLOCK, GROUP_COLS), 1)
    causal = kpos <= (col % MOBA_BLOCK)
    second = ((col // MOBA_BLOCK) % Q_SUB) == 1
    neg = jnp.float32(-jnp.inf)
    tiles = [jnp.where(causal | second, 0.0, neg), jnp.where(causal, 0.0, neg)]
    return jnp.stack(tiles).astype(_BF16)


def _flash_blocks(step, qcats, k_ref, vt_ref, bias_ref, s_refs, p_ref, a_ref, m_ref, acc_ref,
                  active_fn, all_past_active, split_heads):
    groups = len(qcats)
    n_chunks = KEY_BLOCK // BF16_ROWS
    ones = jnp.ones((BF16_ROWS, KEY_BLOCK), _BF16)
    own = 2 * step

    def scores(j, slot):
        kb = k_ref[0, pl.ds(pl.multiple_of(j * KEY_BLOCK, KEY_BLOCK), KEY_BLOCK), :]
        for g in range(groups):
            s_refs[slot][:, g * GROUP_COLS:(g + 1) * GROUP_COLS] = jnp.dot(
                kb[:, g * LANES:(g + 1) * LANES], qcats[g],
                preferred_element_type=_F32).astype(_BF16)

    def softmax(j, slot, bias_tile=None):
        s_ref = s_refs[slot]
        cols = s_ref.shape[1]

        def rows(c):
            r = slice(c * BF16_ROWS, (c + 1) * BF16_ROWS)
            if bias_tile is None:
                return s_ref[r, :]
            bias = bias_ref[bias_tile, r, :]
            return jnp.concatenate([s_ref[r, g * GROUP_COLS:(g + 1) * GROUP_COLS] + bias
                                    for g in range(groups)], axis=1)

        mx = rows(0)
        for c in range(1, n_chunks):
            mx = jnp.maximum(mx, rows(c))
        bmax = jnp.max(mx.astype(_F32), axis=0, keepdims=True)
        m_old = m_ref[...]
        if bias_tile is None and all_past_active:
            m_new = jnp.maximum(m_old, bmax)
            m_used = m_new
        else:
            active = active_fn(j)
            m_new = jnp.where(active, jnp.maximum(m_old, bmax), m_old)
            m_used = jnp.where(active, m_new, jnp.inf)
        m_ref[...] = m_new
        a_ref[slot] = jnp.exp2(m_old - m_new)
        m_tile = jnp.broadcast_to(m_used, (BF16_ROWS, cols)).astype(_BF16)
        for c in range(n_chunks):
            p_ref[slot, c * BF16_ROWS:(c + 1) * BF16_ROWS, :] = jnp.exp2(rows(c) - m_tile)

    def values(j, slot):
        alpha = a_ref[slot]
        for g in range(groups):
            c0 = g * GROUP_COLS
            vt = vt_ref[0, j, g * LANES:(g + 1) * LANES, :]
            if split_heads:
                for h in range(2):
                    cols = slice(c0 + h * Q_BLOCK, c0 + (h + 1) * Q_BLOCK)
                    lhs = jnp.concatenate([vt[h * HEAD_DIM:(h + 1) * HEAD_DIM], ones], axis=0)
                    acc_ref[g, :, h * Q_BLOCK:(h + 1) * Q_BLOCK] = (
                        alpha[:, cols] * acc_ref[g, :, h * Q_BLOCK:(h + 1) * Q_BLOCK]
                        + jnp.dot(lhs, p_ref[slot, :, cols], preferred_element_type=_F32))
            else:
                cols = slice(c0, c0 + GROUP_COLS)
                lhs = jnp.concatenate([vt, ones], axis=0)
                acc_ref[g] = alpha[:, cols] * acc_ref[g] + jnp.dot(
                    lhs, p_ref[slot, :, cols], preferred_element_type=_F32)

    m_ref[...] = jnp.full(m_ref.shape, NEG_INIT, _F32)
    acc_ref[...] = jnp.zeros(acc_ref.shape, _F32)
    a_ref[1] = jnp.ones(a_ref.shape[1:], _F32)
    p_ref[1] = jnp.zeros(p_ref.shape[1:], _BF16)
    scores(0, 0)

    def past_pair(t, carry):
        j = 2 * t
        scores(j + 1, 1)
        values(jnp.maximum(j - 1, 0), 1)
        softmax(j, 0)
        scores(j + 2, 0)
        values(j, 0)
        softmax(j + 1, 1)
        return carry

    lax.fori_loop(0, step, past_pair, 0)
    scores(own + 1, 1)
    values(jnp.maximum(own - 1, 0), 1)
    softmax(own, 0, bias_tile=0)
    values(own, 0)
    softmax(own + 1, 1, bias_tile=1)
    values(own + 1, 1)


def _attn_scratch(nb_sel, groups, acc_rows):
    cols = groups * GROUP_COLS
    shapes = [pltpu.VMEM((KEY_BLOCK, cols), _BF16), pltpu.VMEM((KEY_BLOCK, cols), _BF16),
              pltpu.VMEM((2, KEY_BLOCK, cols), _BF16),
              pltpu.VMEM((2, 1, cols), _F32)]
    if nb_sel:
        shapes.append(pltpu.VMEM((nb_sel, cols), _F32))
    shapes += [pltpu.VMEM((1, cols), _F32),
               pltpu.VMEM((groups, acc_rows, GROUP_COLS), _F32)]
    return shapes


def _moba_body(q_ref, k_ref, vt_ref, kmean_ref, bias_ref, o_ref, s0_ref, s1_ref, p_ref, a_ref,
               sel_ref, m_ref, acc_ref, *, nb, groups):
    step = pl.program_id(2)
    blk = lax.broadcasted_iota(jnp.int32, (nb, GROUP_COLS), 0)
    own_blk = 2 * step + _query_sub_block(GROUP_COLS)
    valid = blk < own_blk
    blk_f = blk.astype(_F32)
    qcats = []
    for g in range(groups):
        qcat = _stacked_queries(q_ref, g)
        qcats.append(qcat)
        kmean = kmean_ref[0, :, g * LANES:(g + 1) * LANES].astype(_BF16)
        gate = jnp.dot(kmean, qcat, preferred_element_type=_F32)
        gate = jnp.where(valid, gate, -jnp.inf)
        sel = jnp.where(blk == own_blk, 1.0, 0.0)
        for _ in range(MOBA_TOPK):
            mx = jnp.max(gate, axis=0, keepdims=True)
            idx = jnp.min(jnp.where(gate == mx, blk_f, float(nb)), axis=0, keepdims=True)
            pick = (blk_f == idx) & valid
            sel = jnp.where(pick, 1.0, sel)
            gate = jnp.where(pick, -jnp.inf, gate)
        sel_ref[:, g * GROUP_COLS:(g + 1) * GROUP_COLS] = sel

    def active_fn(j):
        return sel_ref[pl.ds(j, 1), :] > 0.0

    _flash_blocks(step, qcats, k_ref, vt_ref, bias_ref, (s0_ref, s1_ref), p_ref, a_ref, m_ref,
                  acc_ref, active_fn, all_past_active=False, split_heads=True)

    for g in range(groups):
        acc = acc_ref[g]
        o = acc[:HEAD_DIM] / acc[HEAD_DIM:HEAD_DIM + 1]
        o_t = jnp.concatenate([o[:, :Q_BLOCK], o[:, Q_BLOCK:]], axis=0)
        o_ref[0, :, g * LANES:(g + 1) * LANES] = o_t.T.astype(o_ref.dtype)


def _moba_attention(qk, vt, kmean, bias, *, width, groups):
    B, S, _ = qk.shape
    nb = S // KEY_BLOCK
    gw = groups * LANES
    chunks = width // gw
    body = functools.partial(_moba_body, nb=nb, groups=groups)
    return pl.pallas_call(
        body,
        grid=(B, chunks, S // Q_BLOCK),
        in_specs=[
            pl.BlockSpec((1, Q_BLOCK, gw), lambda b, p, i: (b, i, p)),
            pl.BlockSpec((1, S, gw), lambda b, p, i: (b, 0, chunks + p)),
            pl.BlockSpec((1, nb, gw, KEY_BLOCK), lambda b, p, i: (b, 0, p, 0)),
            pl.BlockSpec((1, nb, gw), lambda b, p, i: (b, 0, p)),
            _const_spec(bias.shape),
        ],
        out_specs=pl.BlockSpec((1, Q_BLOCK, gw), lambda b, p, i: (b, i, p)),
        out_shape=jax.ShapeDtypeStruct((B, S, width), _BF16),
        scratch_shapes=_attn_scratch(nb, groups, HEAD_DIM + BF16_ROWS),
        compiler_params=pltpu.CompilerParams(
            dimension_semantics=("parallel", "parallel", "arbitrary"),
            vmem_limit_bytes=VMEM_LIMIT_BYTES),
        name="moba_attention",
    )(qk, qk, vt, kmean, bias)


def _diff_body(q_ref, k_ref, vt_ref, bias_ref, lq1_ref, lk1_ref, lq2_ref, lk2_ref, subln_ref, o_ref,
               s0_ref, s1_ref, p_ref, a_ref, m_ref, acc_ref, *, lambda_init, groups):
    step = pl.program_id(2)
    qcats = [_stacked_queries(q_ref, g) for g in range(groups)]
    own_blk = 2 * step + _query_sub_block(groups * GROUP_COLS)

    def active_fn(j):
        return j <= own_blk

    _flash_blocks(step, qcats, k_ref, vt_ref, bias_ref, (s0_ref, s1_ref), p_ref, a_ref, m_ref,
                  acc_ref, active_fn, all_past_active=True, split_heads=False)

    lam = (jnp.exp(jnp.sum(lq1_ref[...] * lk1_ref[...], axis=-1, keepdims=True))
           - jnp.exp(jnp.sum(lq2_ref[...] * lk2_ref[...], axis=-1, keepdims=True))
           + lambda_init)
    for g in range(groups):
        acc = acc_ref[g]
        o = acc[:LANES] / acc[LANES:LANES + 1]
        a_t = o[:, :Q_BLOCK] - lam * o[:, Q_BLOCK:]
        ms = jnp.mean(a_t * a_t, axis=0, keepdims=True)
        a = (a_t * lax.rsqrt(ms + RMS_EPS)).T
        out = (a * subln_ref[...]) * (1.0 - lambda_init)
        o_ref[0, :, g * LANES:(g + 1) * LANES] = out.astype(o_ref.dtype)


def _diff_attention(qk, vt, bias, lq1, lk1, lq2, lk2, subln, *, width, lambda_init, groups):
    B, S, _ = qk.shape
    nb = S // KEY_BLOCK
    gw = groups * LANES
    chunks = width // gw
    body = functools.partial(_diff_body, lambda_init=lambda_init, groups=groups)
    vec = lambda n: _const_spec((1, n))
    return pl.pallas_call(
        body,
        grid=(B, chunks, S // Q_BLOCK),
        in_specs=[
            pl.BlockSpec((1, Q_BLOCK, gw), lambda b, h, i: (b, i, 2 * chunks + h)),
            pl.BlockSpec((1, S, gw), lambda b, h, i: (b, 0, 3 * chunks + h)),
            pl.BlockSpec((1, nb, gw, KEY_BLOCK), lambda b, h, i: (b, 0, chunks + h, 0)),
            _const_spec(bias.shape),
            vec(HEAD_DIM), vec(HEAD_DIM), vec(HEAD_DIM), vec(HEAD_DIM), vec(LANES),
        ],
        out_specs=pl.BlockSpec((1, Q_BLOCK, gw), lambda b, h, i: (b, i, h)),
        out_shape=jax.ShapeDtypeStruct((B, S, width), _BF16),
        scratch_shapes=_attn_scratch(0, groups, LANES + BF16_ROWS),
        compiler_params=pltpu.CompilerParams(
            dimension_semantics=("parallel", "parallel", "arbitrary"),
            vmem_limit_bytes=VMEM_LIMIT_BYTES),
        name="diff_attention",
    )(qk, qk, vt, bias, lq1, lk1, lq2, lk2, subln)


def _out_ffn_body(x_ref, mo_ref, do_ref, wo_ref, gain_ref, wg_ref, wu_ref, wd_ref, y_ref, *, width):
    x = x_ref[...]
    att = (jnp.dot(mo_ref[...], wo_ref[0:width, :], preferred_element_type=_F32)
           + jnp.dot(do_ref[...], wo_ref[width:2 * width, :], preferred_element_type=_F32))
    x1 = x + att
    ms = jnp.mean(x1 * x1, axis=-1, keepdims=True)
    h = (x1 * lax.rsqrt(ms + RMS_EPS) * gain_ref[...]).astype(_BF16)
    g = jnp.dot(h, wg_ref[...], preferred_element_type=_F32)
    u = jnp.dot(h, wu_ref[...], preferred_element_type=_F32)
    a = (g * jax.nn.sigmoid(g) * u).astype(_BF16)
    y_ref[...] = x1 + jnp.dot(a, wd_ref[...], preferred_element_type=_F32)


def _out_ffn(x2, mo2, do2, wo, gain, wg, wu, wd, *, tm):
    T, D = x2.shape
    width = mo2.shape[1]
    F = wg.shape[1]
    body = functools.partial(_out_ffn_body, width=width)
    row = lambda n: pl.BlockSpec((tm, n), lambda t: (t, 0))
    return pl.pallas_call(
        body,
        grid=(T // tm,),
        in_specs=[row(D), row(width), row(width), _const_spec((2 * width, D)), _const_spec((1, D)),
                  _const_spec((D, F)), _const_spec((D, F)), _const_spec((F, D))],
        out_specs=row(D),
        out_shape=jax.ShapeDtypeStruct((T, D), _F32),
        compiler_params=pltpu.CompilerParams(
            dimension_semantics=("parallel",), vmem_limit_bytes=VMEM_LIMIT_BYTES),
        name="outproj_swiglu",
    )(x2, mo2, do2, wo, gain, wg, wu, wd)


def _qk_lane_order(a):
    half = HEAD_DIM // 2
    lead = a.shape[:-1]
    a = a.reshape(*lead, a.shape[-1] // LANES, 2, 2, half)
    return jnp.swapaxes(a, -3, -2).reshape(*lead, -1)


def _rope_tables(seq):
    inv = 1.0 / (ROPE_THETA ** (jnp.arange(0, HEAD_DIM, 2, dtype=_F32) / HEAD_DIM))
    ang = jnp.arange(seq, dtype=_F32)[:, None] * inv[None, :]
    cos = jnp.tile(jnp.cos(ang), (1, 2 * LANES // HEAD_DIM))
    sin = jnp.sin(ang)
    sin = jnp.concatenate([-sin, -sin, sin, sin], axis=-1)
    return cos, sin


def kernel(x, attn_norm, w_in, moba_q_norm, moba_k_norm, diff_q_norm, diff_k_norm,
           lambda_q1, lambda_k1, lambda_q2, lambda_k2, diff_subln, w_out,
           ffn_norm, w_gate, w_up, w_down):
    B, S, D = x.shape
    depth = w_in.shape[0]
    width = w_in.shape[2] // 6
    heads_per_group = width // HEAD_DIM
    tm_in = Q_BLOCK
    tm_ffn = Q_BLOCK
    cos, sin = _rope_tables(S)
    bias = _causal_bias()
    gmat = jnp.kron(jnp.eye(MXU_COLS // HEAD_DIM, dtype=_F32),
                    jnp.full((HEAD_DIM, HEAD_DIM), 1.0 / HEAD_DIM, _F32))
    gmat = _qk_lane_order(_qk_lane_order(gmat).T).astype(_BF16)
    scale = HEAD_DIM ** -0.5 * math.log2(math.e)
    for l in range(depth):
        lambda_init = 0.8 - 0.6 * math.exp(-0.3 * l)
        tile = lambda g: _qk_lane_order(jnp.tile(g.astype(_F32), heads_per_group))
        hgain = jnp.stack([tile(moba_q_norm[l]) * scale, tile(moba_k_norm[l]),
                           tile(diff_q_norm[l]) * scale, tile(diff_k_norm[l])])
        w = w_in[l].astype(_BF16).reshape(D, 6, width)
        w = jnp.concatenate([_qk_lane_order(w[:, :2]), w[:, 2:3], _qk_lane_order(w[:, 3:5]),
                             w[:, 5:]], axis=1).reshape(D, 6 * width)
        qk, vt, kmean = _inproj(x, attn_norm[l][None, :], w, gmat, hgain, cos, sin, tm=tm_in)
        kmean = kmean.reshape(B, S // MOBA_BLOCK, width)
        moba_out = _moba_attention(qk, vt, kmean, bias, width=width, groups=ATTN_GROUPS)
        diff_out = _diff_attention(qk, vt, bias, lambda_q1[l][None, :], lambda_k1[l][None, :],
                                   lambda_q2[l][None, :], lambda_k2[l][None, :],
                                   diff_subln[l][None, :], width=width, lambda_init=lambda_init,
                                   groups=ATTN_GROUPS)
        y = _out_ffn(x.reshape(B * S, D), moba_out.reshape(B * S, width),
                     diff_out.reshape(B * S, width), w_out[l].astype(_BF16), ffn_norm[l][None, :],
                     w_gate[l].astype(_BF16), w_up[l].astype(_BF16), w_down[l].astype(_BF16),
                     tm=tm_ffn)
        x = y.reshape(B, S, D)
    return x
```

```python
import functools
import math

import jax
import jax.numpy as jnp
from jax import lax
from jax.experimental import pallas as pl
from jax.experimental.pallas import tpu as pltpu

HEAD_DIM = 64
LANES = 128
BF16_ROWS = 16
MXU_COLS = 256
MOBA_BLOCK = 256
MOBA_TOPK = 3
ROPE_THETA = 10000.0
RMS_EPS = 1e-6
KEY_BLOCK = MOBA_BLOCK
Q_SUB = 2
Q_BLOCK = Q_SUB * MOBA_BLOCK
ATTN_GROUPS = 4
GROUP_COLS = 2 * Q_BLOCK
NEG_INIT = -1e30
VMEM_LIMIT_BYTES = 56 * 1024 * 1024

_F32 = jnp.float32
_BF16 = jnp.bfloat16


def _const_spec(shape):
    zeros = (0,) * len(shape)
    return pl.BlockSpec(shape, lambda *_: zeros, pipeline_mode=pl.Buffered(1))


def _inproj_body(x_ref, gain_ref, wmq_ref, wmk_ref, wmv_ref, wdq_ref, wdk_ref, wdv_ref, gmat_ref,
                 hgain_ref, cos_ref, sin_ref, qk_ref, vt_ref, kmean_ref, *, width):
    x = x_ref[0]
    ms = jnp.mean(x * x, axis=-1, keepdims=True)
    h = (x * lax.rsqrt(ms + RMS_EPS) * gain_ref[...]).astype(_BF16)

    def project(w_ref):
        return jnp.dot(h, w_ref[...], preferred_element_type=_F32)

    tm = x.shape[0]
    cos = cos_ref[...]
    sin = sin_ref[...]

    for gi, w_ref in enumerate((wmq_ref, wmk_ref, wdq_ref, wdk_ref)):
        p = project(w_ref)
        sq = (p * p).astype(_BF16)
        msq = jnp.concatenate(
            [jnp.dot(sq[:, c:c + MXU_COLS], gmat_ref[...], preferred_element_type=_F32)
             for c in range(0, width, MXU_COLS)], axis=1)
        y = p * lax.rsqrt(msq + RMS_EPS) * hgain_ref[gi:gi + 1, :]
        for cc in range(width // LANES):
            yc = y[:, cc * LANES:(cc + 1) * LANES]
            oc = yc * cos + pltpu.roll(yc, LANES // 2, 1) * sin
            qk_ref[0, :, gi * width + cc * LANES:gi * width + (cc + 1) * LANES] = oc.astype(_BF16)
            if gi == 1:
                for rb in range(tm // MOBA_BLOCK):
                    kmean_ref[0, 0, rb:rb + 1, cc * LANES:(cc + 1) * LANES] = jnp.mean(
                        oc[rb * MOBA_BLOCK:(rb + 1) * MOBA_BLOCK], axis=0, keepdims=True)

    for vi, w_ref in enumerate((wmv_ref, wdv_ref)):
        v = project(w_ref)
        for rb in range(tm // KEY_BLOCK):
            vt_ref[0, rb, vi * width:(vi + 1) * width, :] = (
                v[rb * KEY_BLOCK:(rb + 1) * KEY_BLOCK].T.astype(_BF16))


def _inproj(x, gain, w_groups, gmat, hgain, cos, sin, *, tm):
    B, S, D = x.shape
    width = w_groups[0].shape[1]
    body = functools.partial(_inproj_body, width=width)
    return pl.pallas_call(
        body,
        grid=(B, S // tm),
        in_specs=[
            pl.BlockSpec((1, tm, D), lambda b, t: (b, t, 0)),
            _const_spec((1, D)),
            *[_const_spec((D, width)) for _ in w_groups],
            _const_spec((MXU_COLS, MXU_COLS)),
            _const_spec((4, width)),
            pl.BlockSpec((tm, LANES), lambda b, t: (t, 0)),
            pl.BlockSpec((tm, LANES), lambda b, t: (t, 0)),
        ],
        out_specs=[
            pl.BlockSpec((1, tm, 4 * width), lambda b, t: (b, t, 0)),
            pl.BlockSpec((1, tm // KEY_BLOCK, 2 * width, KEY_BLOCK), lambda b, t: (b, t, 0, 0)),
            pl.BlockSpec((1, 1, tm // MOBA_BLOCK, width), lambda b, t: (b, t, 0, 0)),
        ],
        out_shape=[
            jax.ShapeDtypeStruct((B, S, 4 * width), _BF16),
            jax.ShapeDtypeStruct((B, S // KEY_BLOCK, 2 * width, KEY_BLOCK), _BF16),
            jax.ShapeDtypeStruct((B, S // tm, tm // MOBA_BLOCK, width), _F32),
        ],
        compiler_params=pltpu.CompilerParams(
            dimension_semantics=("parallel", "parallel"), vmem_limit_bytes=VMEM_LIMIT_BYTES),
        name="inproj_qknorm_rope",
    )(x, gain, *w_groups, gmat, hgain, cos, sin)


def _stacked_queries(q_ref, g):
    q = q_ref[0, :, g * LANES:(g + 1) * LANES]
    lane = lax.broadcasted_iota(jnp.int32, q.shape, 1)
    zero = jnp.zeros_like(q)
    first = (lane // (HEAD_DIM // 2)) % 2 == 0
    stacked = jnp.concatenate([jnp.where(first, q, zero), jnp.where(first, zero, q)], axis=0)
    return stacked.astype(_F32).T.astype(_BF16)


def _query_sub_block(cols):
    col = lax.broadcasted_iota(jnp.int32, (1, cols), 1)
    return (col // MOBA_BLOCK) % Q_SUB


def _causal_bias():
    kpos = lax.broadcasted_iota(jnp.int32, (KEY_BLOCK, GROUP_COLS), 0)
    col = lax.broadcasted_iota(jnp.int32, (KEY_BLOCK, GROUP_COLS), 1)
    causal = kpos <= (col % MOBA_BLOCK)
    second = ((col // MOBA_BLOCK) % Q_SUB) == 1
    neg = jnp.float32(-jnp.inf)
    tiles = [jnp.where(causal | second, 0.0, neg), jnp.where(causal, 0.0, neg)]
    return jnp.stack(tiles).astype(_BF16)


def _flash_blocks(step, qcats, k_ref, vt_ref, bias_ref, s_refs, p_ref, a_ref, m_ref, acc_ref,
                  active_fn, all_past_active, split_heads):
    groups = len(qcats)
    n_chunks = KEY_BLOCK // BF16_ROWS
    ones = jnp.ones((BF16_ROWS, KEY_BLOCK), _BF16)
    own = 2 * step

    def scores(j, slot):
        kb = k_ref[0, pl.ds(pl.multiple_of(j * KEY_BLOCK, KEY_BLOCK), KEY_BLOCK), :]
        for g in range(groups):
            s_refs[slot][:, g * GROUP_COLS:(g + 1) * GROUP_COLS] = jnp.dot(
                kb[:, g * LANES:(g + 1) * LANES], qcats[g],
                preferred_element_type=_F32).astype(_BF16)

    def softmax(j, slot, bias_tile=None):
        s_ref = s_refs[slot]
        cols = s_ref.shape[1]

        def rows(c):
            r = slice(c * BF16_ROWS, (c + 1) * BF16_ROWS)
            if bias_tile is None:
                return s_ref[r, :]
            bias = bias_ref[bias_tile, r, :]
            return jnp.concatenate([s_ref[r, g * GROUP_COLS:(g + 1) * GROUP_COLS] + bias
                                    for g in range(groups)], axis=1)

        mx = rows(0)
        for c in range(1, n_chunks):
            mx = jnp.maximum(mx, rows(c))
        bmax = jnp.max(mx.astype(_F32), axis=0, keepdims=True)
        m_old = m_ref[...]
        if bias_tile is None and all_past_active:
            m_new = jnp.maximum(m_old, bmax)
            m_used = m_new
        else:
            active = active_fn(j)
            m_new = jnp.where(active, jnp.maximum(m_old, bmax), m_old)
            m_used = jnp.where(active, m_new, jnp.inf)
        m_ref[...] = m_new
        a_ref[slot] = jnp.exp2(m_old - m_new)
        m_tile = jnp.broadcast_to(m_used, (BF16_ROWS, cols)).astype(_BF16)
        for c in range(n_chunks):
            p_ref[slot, c * BF16_ROWS:(c + 1) * BF16_ROWS, :] = jnp.exp2(rows(c) - m_tile)

    def values(j, slot):
        alpha = a_ref[slot]
        for g in range(groups):
            c0 = g * GROUP_COLS
            vt = vt_ref[0, j, g * LANES:(g + 1) * LANES, :]
            if split_heads:
                for h in range(2):
                    cols = slice(c0 + h * Q_BLOCK, c0 + (h + 1) * Q_BLOCK)
                    lhs = jnp.concatenate([vt[h * HEAD_DIM:(h + 1) * HEAD_DIM], ones], axis=0)
                    acc_ref[g, :, h * Q_BLOCK:(h + 1) * Q_BLOCK] = (
                        alpha[:, cols] * acc_ref[g, :, h * Q_BLOCK:(h + 1) * Q_BLOCK]
                        + jnp.dot(lhs, p_ref[slot, :, cols], preferred_element_type=_F32))
            else:
                cols = slice(c0, c0 + GROUP_COLS)
                lhs = jnp.concatenate([vt, ones], axis=0)
                acc_ref[g] = alpha[:, cols] * acc_ref[g] + jnp.dot(
                    lhs, p_ref[slot, :, cols], preferred_element_type=_F32)

    m_ref[...] = jnp.full(m_ref.shape, NEG_INIT, _F32)
    acc_ref[...] = jnp.zeros(acc_ref.shape, _F32)
    a_ref[1] = jnp.ones(a_ref.shape[1:], _F32)
    p_ref[1] = jnp.zeros(p_ref.shape[1:], _BF16)
    scores(0, 0)

    def past_pair(t, carry):
        j = 2 * t
        scores(j + 1, 1)
        values(jnp.maximum(j - 1, 0), 1)
        softmax(j, 0)
        scores(j + 2, 0)
        values(j, 0)
        softmax(j + 1, 1)
        return carry

    lax.fori_loop(0, step, past_pair, 0)
    scores(own + 1, 1)
    values(jnp.maximum(own - 1, 0), 1)
    softmax(own, 0, bias_tile=0)
    values(own, 0)
    softmax(own + 1, 1, bias_tile=1)
    values(own + 1, 1)


def _attn_scratch(nb_sel, groups, acc_rows):
    cols = groups * GROUP_COLS
    shapes = [pltpu.VMEM((KEY_BLOCK, cols), _BF16), pltpu.VMEM((KEY_BLOCK, cols), _BF16),
              pltpu.VMEM((2, KEY_BLOCK, cols), _BF16),
              pltpu.VMEM((2, 1, cols), _F32)]
    if nb_sel:
        shapes.append(pltpu.VMEM((nb_sel, cols), _F32))
    shapes += [pltpu.VMEM((1, cols), _F32),
               pltpu.VMEM((groups, acc_rows, GROUP_COLS), _F32)]
    return shapes


def _moba_body(q_ref, k_ref, vt_ref, kmean_ref, bias_ref, o_ref, s0_ref, s1_ref, p_ref, a_ref,
               sel_ref, m_ref, acc_ref, *, nb, groups):
    step = pl.program_id(2)
    blk = lax.broadcasted_iota(jnp.int32, (nb, GROUP_COLS), 0)
    own_blk = 2 * step + _query_sub_block(GROUP_COLS)
    valid = blk < own_blk
    blk_f = blk.astype(_F32)
    qcats = []
    for g in range(groups):
        qcat = _stacked_queries(q_ref, g)
        qcats.append(qcat)
        kmean = kmean_ref[0, :, g * LANES:(g + 1) * LANES].astype(_BF16)
        gate = jnp.dot(kmean, qcat, preferred_element_type=_F32)
        gate = jnp.where(valid, gate, -jnp.inf)
        sel = jnp.where(blk == own_blk, 1.0, 0.0)
        for _ in range(MOBA_TOPK):
            mx = jnp.max(gate, axis=0, keepdims=True)
            idx = jnp.min(jnp.where(gate == mx, blk_f, float(nb)), axis=0, keepdims=True)
            pick = (blk_f == idx) & valid
            sel = jnp.where(pick, 1.0, sel)
            gate = jnp.where(pick, -jnp.inf, gate)
        sel_ref[:, g * GROUP_COLS:(g + 1) * GROUP_COLS] = sel

    def active_fn(j):
        return sel_ref[pl.ds(j, 1), :] > 0.0

    _flash_blocks(step, qcats, k_ref, vt_ref, bias_ref, (s0_ref, s1_ref), p_ref, a_ref, m_ref,
                  acc_ref, active_fn, all_past_active=False, split_heads=True)

    for g in range(groups):
        acc = acc_ref[g]
        o = acc[:HEAD_DIM] / acc[HEAD_DIM:HEAD_DIM + 1]
        o_t = jnp.concatenate([o[:, :Q_BLOCK], o[:, Q_BLOCK:]], axis=0)
        o_ref[0, :, g * LANES:(g + 1) * LANES] = o_t.T.astype(o_ref.dtype)


def _moba_attention(qk, vt, kmean, bias, *, width, groups):
    B, S, _ = qk.shape
    nb = S // KEY_BLOCK
    gw = groups * LANES
    chunks = width // gw
    body = functools.partial(_moba_body, nb=nb, groups=groups)
    return pl.pallas_call(
        body,
        grid=(B, chunks, S // Q_BLOCK),
        in_specs=[
            pl.BlockSpec((1, Q_BLOCK, gw), lambda b, p, i: (b, i, p)),
            pl.BlockSpec((1, S, gw), lambda b, p, i: (b, 0, chunks + p)),
            pl.BlockSpec((1, nb, gw, KEY_BLOCK), lambda b, p, i: (b, 0, p, 0)),
            pl.BlockSpec((1, nb, gw), lambda b, p, i: (b, 0, p)),
            _const_spec(bias.shape),
        ],
        out_specs=pl.BlockSpec((1, Q_BLOCK, gw), lambda b, p, i: (b, i, p)),
        out_shape=jax.ShapeDtypeStruct((B, S, width), _BF16),
        scratch_shapes=_attn_scratch(nb, groups, HEAD_DIM + BF16_ROWS),
        compiler_params=pltpu.CompilerParams(
            dimension_semantics=("parallel", "parallel", "arbitrary"),
            vmem_limit_bytes=VMEM_LIMIT_BYTES),
        name="moba_attention",
    )(qk, qk, vt, kmean, bias)


def _diff_body(q_ref, k_ref, vt_ref, bias_ref, lq1_ref, lk1_ref, lq2_ref, lk2_ref, subln_ref, o_ref,
               s0_ref, s1_ref, p_ref, a_ref, m_ref, acc_ref, *, lambda_init, groups):
    step = pl.program_id(2)
    qcats = [_stacked_queries(q_ref, g) for g in range(groups)]
    own_blk = 2 * step + _query_sub_block(groups * GROUP_COLS)

    def active_fn(j):
        return j <= own_blk

    _flash_blocks(step, qcats, k_ref, vt_ref, bias_ref, (s0_ref, s1_ref), p_ref, a_ref, m_ref,
                  acc_ref, active_fn, all_past_active=True, split_heads=False)

    lam = (jnp.exp(jnp.sum(lq1_ref[...] * lk1_ref[...], axis=-1, keepdims=True))
           - jnp.exp(jnp.sum(lq2_ref[...] * lk2_ref[...], axis=-1, keepdims=True))
           + lambda_init)
    for g in range(groups):
        acc = acc_ref[g]
        o = acc[:LANES] / acc[LANES:LANES + 1]
        a_t = o[:, :Q_BLOCK] - lam * o[:, Q_BLOCK:]
        ms = jnp.mean(a_t * a_t, axis=0, keepdims=True)
        a = (a_t * lax.rsqrt(ms + RMS_EPS)).T
        out = (a * subln_ref[...]) * (1.0 - lambda_init)
        o_ref[0, :, g * LANES:(g + 1) * LANES] = out.astype(o_ref.dtype)


def _diff_attention(qk, vt, bias, lq1, lk1, lq2, lk2, subln, *, width, lambda_init, groups):
    B, S, _ = qk.shape
    nb = S // KEY_BLOCK
    gw = groups * LANES
    chunks = width // gw
    body = functools.partial(_diff_body, lambda_init=lambda_init, groups=groups)
    vec = lambda n: _const_spec((1, n))
    return pl.pallas_call(
        body,
        grid=(B, chunks, S // Q_BLOCK),
        in_specs=[
            pl.BlockSpec((1, Q_BLOCK, gw), lambda b, h, i: (b, i, 2 * chunks + h)),
            pl.BlockSpec((1, S, gw), lambda b, h, i: (b, 0, 3 * chunks + h)),
            pl.BlockSpec((1, nb, gw, KEY_BLOCK), lambda b, h, i: (b, 0, chunks + h, 0)),
            _const_spec(bias.shape),
            vec(HEAD_DIM), vec(HEAD_DIM), vec(HEAD_DIM), vec(HEAD_DIM), vec(LANES),
        ],
        out_specs=pl.BlockSpec((1, Q_BLOCK, gw), lambda b, h, i: (b, i, h)),
        out_shape=jax.ShapeDtypeStruct((B, S, width), _BF16),
        scratch_shapes=_attn_scratch(0, groups, LANES + BF16_ROWS),
        compiler_params=pltpu.CompilerParams(
            dimension_semantics=("parallel", "parallel", "arbitrary"),
            vmem_limit_bytes=VMEM_LIMIT_BYTES),
        name="diff_attention",
    )(qk, qk, vt, bias, lq1, lk1, lq2, lk2, subln)


def _out_ffn_body(x_ref, mo_ref, do_ref, wo_ref, gain_ref, wg_ref, wu_ref, wd_ref, y_ref, *, width):
    x = x_ref[...]
    att = (jnp.dot(mo_ref[...], wo_ref[0:width, :], preferred_element_type=_F32)
           + jnp.dot(do_ref[...], wo_ref[width:2 * width, :], preferred_element_type=_F32))
    x1 = x + att
    ms = jnp.mean(x1 * x1, axis=-1, keepdims=True)
    h = (x1 * lax.rsqrt(ms + RMS_EPS) * gain_ref[...]).astype(_BF16)
    g = jnp.dot(h, wg_ref[...], preferred_element_type=_F32)
    u = jnp.dot(h, wu_ref[...], preferred_element_type=_F32)
    a = (g * jax.nn.sigmoid(g) * u).astype(_BF16)
    y_ref[...] = x1 + jnp.dot(a, wd_ref[...], preferred_element_type=_F32)


def _out_ffn(x2, mo2, do2, wo, gain, wg, wu, wd, *, tm):
    T, D = x2.shape
    width = mo2.shape[1]
    F = wg.shape[1]
    body = functools.partial(_out_ffn_body, width=width)
    row = lambda n: pl.BlockSpec((tm, n), lambda t: (t, 0))
    return pl.pallas_call(
        body,
        grid=(T // tm,),
        in_specs=[row(D), row(width), row(width), _const_spec((2 * width, D)), _const_spec((1, D)),
                  _const_spec((D, F)), _const_spec((D, F)), _const_spec((F, D))],
        out_specs=row(D),
        out_shape=jax.ShapeDtypeStruct((T, D), _F32),
        compiler_params=pltpu.CompilerParams(
            dimension_semantics=("parallel",), vmem_limit_bytes=VMEM_LIMIT_BYTES),
        name="outproj_swiglu",
    )(x2, mo2, do2, wo, gain, wg, wu, wd)


def _qk_lane_order(a):
    half = HEAD_DIM // 2
    lead = a.shape[:-1]
    a = a.reshape(*lead, a.shape[-1] // LANES, 2, 2, half)
    return jnp.swapaxes(a, -3, -2).reshape(*lead, -1)


def _rope_tables(seq):
    inv = 1.0 / (ROPE_THETA ** (jnp.arange(0, HEAD_DIM, 2, dtype=_F32) / HEAD_DIM))
    ang = jnp.arange(seq, dtype=_F32)[:, None] * inv[None, :]
    cos = jnp.tile(jnp.cos(ang), (1, 2 * LANES // HEAD_DIM))
    sin = jnp.sin(ang)
    sin = jnp.concatenate([-sin, -sin, sin, sin], axis=-1)
    return cos, sin


def kernel(x, attn_norm, w_in, moba_q_norm, moba_k_norm, diff_q_norm, diff_k_norm,
           lambda_q1, lambda_k1, lambda_q2, lambda_k2, diff_subln, w_out,
           ffn_norm, w_gate, w_up, w_down):
    B, S, D = x.shape
    depth = w_in.shape[0]
    width = w_in.shape[2] // 6
    heads_per_group = width // HEAD_DIM
    tm_in = Q_BLOCK
    tm_ffn = Q_BLOCK
    cos, sin = _rope_tables(S)
    bias = _causal_bias()
    gmat = jnp.kron(jnp.eye(MXU_COLS // HEAD_DIM, dtype=_F32),
                    jnp.full((HEAD_DIM, HEAD_DIM), 1.0 / HEAD_DIM, _F32))
    gmat = _qk_lane_order(_qk_lane_order(gmat).T).astype(_BF16)
    scale = HEAD_DIM ** -0.5 * math.log2(math.e)
    for l in range(depth):
        lambda_init = 0.8 - 0.6 * math.exp(-0.3 * l)
        tile = lambda g: _qk_lane_order(jnp.tile(g.astype(_F32), heads_per_group))
        hgain = jnp.stack([tile(moba_q_norm[l]) * scale, tile(moba_k_norm[l]),
                           tile(diff_q_norm[l]) * scale, tile(diff_k_norm[l])])
        w_groups = [w_in[l][:, i * width:(i + 1) * width] for i in range(6)]
        w_groups = [(w if i in (2, 5) else _qk_lane_order(w)).astype(_BF16)
                    for i, w in enumerate(w_groups)]
        qk, vt, kmean = _inproj(x, attn_norm[l][None, :], w_groups, gmat, hgain, cos, sin,
                                tm=tm_in)
        kmean = kmean.reshape(B, S // MOBA_BLOCK, width)
        moba_out = _moba_attention(qk, vt, kmean, bias, width=width, groups=ATTN_GROUPS)
        diff_out = _diff_attention(qk, vt, bias, lambda_q1[l][None, :], lambda_k1[l][None, :],
                                   lambda_q2[l][None, :], lambda_k2[l][None, :],
                                   diff_subln[l][None, :], width=width, lambda_init=lambda_init,
                                   groups=ATTN_GROUPS)
        y = _out_ffn(x.reshape(B * S, D), moba_out.reshape(B * S, width),
                     diff_out.reshape(B * S, width), w_out[l].astype(_BF16), ffn_norm[l][None, :],
                     w_gate[l].astype(_BF16), w_up[l].astype(_BF16), w_down[l].astype(_BF16),
                     tm=tm_ffn)
        x = y.reshape(B, S, D)
    return x
```

```python
import functools
import math

import jax
import jax.numpy as jnp
from jax import lax
from jax.experimental import pallas as pl
from jax.experimental.pallas import tpu as pltpu

HEAD_DIM = 64
LANES = 128
BF16_ROWS = 16
MXU_COLS = 256
MOBA_BLOCK = 256
MOBA_TOPK = 3
ROPE_THETA = 10000.0
RMS_EPS = 1e-6
KEY_BLOCK = MOBA_BLOCK
Q_SUB = 2
Q_BLOCK = Q_SUB * MOBA_BLOCK
ATTN_GROUPS = 4
GROUP_COLS = 2 * Q_BLOCK
NEG_INIT = -1e30
VMEM_LIMIT_BYTES = 56 * 1024 * 1024

_F32 = jnp.float32
_BF16 = jnp.bfloat16


def _const_spec(shape):
    zeros = (0,) * len(shape)
    return pl.BlockSpec(shape, lambda *_: zeros, pipeline_mode=pl.Buffered(1))


def _inproj_body(x_ref, gain_ref, wmq_ref, wmk_ref, wmv_ref, wdq_ref, wdk_ref, wdv_ref, gmat_ref,
                 hgain_ref, cos_ref, sin_ref, qk_ref, vt_ref, kmean_ref, *, width):
    x = x_ref[0]
    ms = jnp.mean(x * x, axis=-1, keepdims=True)
    h = (x * lax.rsqrt(ms + RMS_EPS) * gain_ref[...]).astype(_BF16)

    def project(w_ref):
        return jnp.dot(h, w_ref[...], preferred_element_type=_F32)

    tm = x.shape[0]
    cos = cos_ref[...]
    sin = sin_ref[...]

    for gi, w_ref in enumerate((wmq_ref, wmk_ref, wdq_ref, wdk_ref)):
        p = project(w_ref)
        sq = (p * p).astype(_BF16)
        msq = jnp.concatenate(
            [jnp.dot(sq[:, c:c + MXU_COLS], gmat_ref[...], preferred_element_type=_F32)
             for c in range(0, width, MXU_COLS)], axis=1)
        y = p * lax.rsqrt(msq + RMS_EPS) * hgain_ref[gi:gi + 1, :]
        for cc in range(width // LANES):
            yc = y[:, cc * LANES:(cc + 1) * LANES]
            oc = yc * cos + pltpu.roll(yc, LANES // 2, 1) * sin
            qk_ref[0, :, gi * width + cc * LANES:gi * width + (cc + 1) * LANES] = oc.astype(_BF16)
            if gi == 1:
                for rb in range(tm // MOBA_BLOCK):
                    kmean_ref[0, 0, rb:rb + 1, cc * LANES:(cc + 1) * LANES] = jnp.mean(
                        oc[rb * MOBA_BLOCK:(rb + 1) * MOBA_BLOCK], axis=0, keepdims=True)

    for vi, w_ref in enumerate((wmv_ref, wdv_ref)):
        v = project(w_ref)
        for rb in range(tm // KEY_BLOCK):
            vt_ref[0, rb, vi * width:(vi + 1) * width, :] = (
                v[rb * KEY_BLOCK:(rb + 1) * KEY_BLOCK].T.astype(_BF16))


def _inproj(x, gain, w_groups, gmat, hgain, cos, sin, *, tm):
    B, S, D = x.shape
    width = w_groups[0].shape[1]
    body = functools.partial(_inproj_body, width=width)
    return pl.pallas_call(
        body,
        grid=(B, S // tm),
        in_specs=[
            pl.BlockSpec((1, tm, D), lambda b, t: (b, t, 0)),
            _const_spec((1, D)),
            *[_const_spec((D, width)) for _ in w_groups],
            _const_spec((MXU_COLS, MXU_COLS)),
            _const_spec((4, width)),
            pl.BlockSpec((tm, LANES), lambda b, t: (t, 0)),
            pl.BlockSpec((tm, LANES), lambda b, t: (t, 0)),
        ],
        out_specs=[
            pl.BlockSpec((1, tm, 4 * width), lambda b, t: (b, t, 0)),
            pl.BlockSpec((1, tm // KEY_BLOCK, 2 * width, KEY_BLOCK), lambda b, t: (b, t, 0, 0)),
            pl.BlockSpec((1, 1, tm // MOBA_BLOCK, width), lambda b, t: (b, t, 0, 0)),
        ],
        out_shape=[
            jax.ShapeDtypeStruct((B, S, 4 * width), _BF16),
            jax.ShapeDtypeStruct((B, S // KEY_BLOCK, 2 * width, KEY_BLOCK), _BF16),
            jax.ShapeDtypeStruct((B, S // tm, tm // MOBA_BLOCK, width), _F32),
        ],
        compiler_params=pltpu.CompilerParams(
            dimension_semantics=("parallel", "parallel"), vmem_limit_bytes=VMEM_LIMIT_BYTES),
        name="inproj_qknorm_rope",
    )(x, gain, *w_groups, gmat, hgain, cos, sin)


def _stacked_queries(q_ref, g):
    q = q_ref[0, :, g * LANES:(g + 1) * LANES]
    lane = lax.broadcasted_iota(jnp.int32, q.shape, 1)
    zero = jnp.zeros_like(q)
    first = (lane // (HEAD_DIM // 2)) % 2 == 0
    stacked = jnp.concatenate([jnp.where(first, q, zero), jnp.where(first, zero, q)], axis=0)
    return stacked.astype(_F32).T.astype(_BF16)


def _query_sub_block(cols):
    col = lax.broadcasted_iota(jnp.int32, (1, cols), 1)
    return (col // MOBA_BLOCK) % Q_SUB


def _causal_bias():
    kpos = lax.broadcasted_iota(jnp.int32, (KEY_BLOCK, GROUP_COLS), 0)
    col = lax.broadcasted_iota(jnp.int32, (KEY_BLOCK, GROUP_COLS), 1)
    causal = kpos <= (col % MOBA_BLOCK)
    second = ((col // MOBA_BLOCK) % Q_SUB) == 1
    neg = jnp.float32(-jnp.inf)
    tiles = [jnp.where(causal | second, 0.0, neg), jnp.where(causal, 0.0, neg)]
    return jnp.stack(tiles).astype(_BF16)


def _flash_blocks(step, qcats, k_ref, vt_ref, bias_ref, s_refs, p_ref, a_ref, m_ref, acc_ref,
                  active_fn, all_past_active, split_heads):
    groups = len(qcats)
    n_chunks = KEY_BLOCK // BF16_ROWS
    ones = jnp.ones((BF16_ROWS, KEY_BLOCK), _BF16)
    own = 2 * step

    group_ranges = [(g * GROUP_COLS, GROUP_COLS) for g in range(groups)]
    second_ranges = [(g * GROUP_COLS + h * Q_BLOCK + MOBA_BLOCK, MOBA_BLOCK)
                     for g in range(groups) for h in range(2)]

    def scores(j, slot, second_only=False):
        kb = k_ref[0, pl.ds(pl.multiple_of(j * KEY_BLOCK, KEY_BLOCK), KEY_BLOCK), :]
        for c0, n in (second_ranges if second_only else group_ranges):
            g, off = divmod(c0, GROUP_COLS)
            s_refs[slot][:, c0:c0 + n] = jnp.dot(
                kb[:, g * LANES:(g + 1) * LANES], qcats[g][:, off:off + n],
                preferred_element_type=_F32).astype(_BF16)

    def softmax(j, slot, bias_tile=None, second_only=False):
        s_ref = s_refs[slot]
        ranges = second_ranges if second_only else group_ranges
        cols = sum(n for _, n in ranges)

        def take(row_ref):
            if not second_only:
                return row_ref[...]
            return jnp.concatenate([row_ref[:, c0:c0 + n] for c0, n in ranges], axis=1)

        def put(dst, value):
            if not second_only:
                dst(slice(None), value)
                return
            at = 0
            for c0, n in ranges:
                dst(slice(c0, c0 + n), value[:, at:at + n])
                at += n

        def rows(c):
            r = slice(c * BF16_ROWS, (c + 1) * BF16_ROWS)
            if bias_tile is None:
                return s_ref[r, :]
            return jnp.concatenate(
                [s_ref[r, c0:c0 + n]
                 + bias_ref[bias_tile, r, c0 % GROUP_COLS:c0 % GROUP_COLS + n] for c0, n in ranges],
                axis=1)

        mx = rows(0)
        for c in range(1, n_chunks):
            mx = jnp.maximum(mx, rows(c))
        bmax = jnp.max(mx.astype(_F32), axis=0, keepdims=True)
        m_old = take(m_ref)
        if bias_tile is None and all_past_active:
            m_new = jnp.maximum(m_old, bmax)
            m_used = m_new
        else:
            active = active_fn(j)
            if second_only:
                active = jnp.concatenate([active[:, c0:c0 + n] for c0, n in ranges], axis=1)
            m_new = jnp.where(active, jnp.maximum(m_old, bmax), m_old)
            m_used = jnp.where(active, m_new, jnp.inf)
        put(lambda cs, v: m_ref.__setitem__((slice(None), cs), v), m_new)
        put(lambda cs, v: a_ref.__setitem__((slot, slice(None), cs), v), jnp.exp2(m_old - m_new))
        m_tile = jnp.broadcast_to(m_used, (BF16_ROWS, cols)).astype(_BF16)
        for c in range(n_chunks):
            r = slice(c * BF16_ROWS, (c + 1) * BF16_ROWS)
            put(lambda cs, v: p_ref.__setitem__((slot, r, cs), v), jnp.exp2(rows(c) - m_tile))

    def values(j, slot, second_only=False):
        alpha = a_ref[slot]
        first = MOBA_BLOCK if second_only else 0
        for g in range(groups):
            vt = vt_ref[0, j, g * LANES:(g + 1) * LANES, :]
            for h in range(2):
                if split_heads:
                    lhs = jnp.concatenate([vt[h * HEAD_DIM:(h + 1) * HEAD_DIM], ones], axis=0)
                    span = (h * Q_BLOCK + first, (h + 1) * Q_BLOCK)
                elif second_only:
                    lhs = jnp.concatenate([vt, ones], axis=0)
                    span = (h * Q_BLOCK + first, (h + 1) * Q_BLOCK)
                elif h == 0:
                    lhs = jnp.concatenate([vt, ones], axis=0)
                    span = (0, GROUP_COLS)
                else:
                    continue
                cols = slice(g * GROUP_COLS + span[0], g * GROUP_COLS + span[1])
                acc_ref[g, :, span[0]:span[1]] = (
                    alpha[:, cols] * acc_ref[g, :, span[0]:span[1]]
                    + jnp.dot(lhs, p_ref[slot, :, cols], preferred_element_type=_F32))

    m_ref[...] = jnp.full(m_ref.shape, NEG_INIT, _F32)
    acc_ref[...] = jnp.zeros(acc_ref.shape, _F32)
    a_ref[1] = jnp.ones(a_ref.shape[1:], _F32)
    p_ref[1] = jnp.zeros(p_ref.shape[1:], _BF16)
    scores(0, 0)

    def past_pair(t, carry):
        j = 2 * t
        scores(j + 1, 1)
        values(jnp.maximum(j - 1, 0), 1)
        softmax(j, 0)
        scores(j + 2, 0)
        values(j, 0)
        softmax(j + 1, 1)
        return carry

    lax.fori_loop(0, step, past_pair, 0)
    scores(own + 1, 1, second_only=True)
    values(jnp.maximum(own - 1, 0), 1)
    softmax(own, 0, bias_tile=0)
    values(own, 0)
    softmax(own + 1, 1, bias_tile=1, second_only=True)
    values(own + 1, 1, second_only=True)


def _attn_scratch(nb_sel, groups, acc_rows):
    cols = groups * GROUP_COLS
    shapes = [pltpu.VMEM((KEY_BLOCK, cols), _BF16), pltpu.VMEM((KEY_BLOCK, cols), _BF16),
              pltpu.VMEM((2, KEY_BLOCK, cols), _BF16),
              pltpu.VMEM((2, 1, cols), _F32)]
    if nb_sel:
        shapes.append(pltpu.VMEM((nb_sel, cols), _F32))
    shapes += [pltpu.VMEM((1, cols), _F32),
               pltpu.VMEM((groups, acc_rows, GROUP_COLS), _F32)]
    return shapes


def _moba_body(q_ref, k_ref, vt_ref, kmean_ref, bias_ref, o_ref, s0_ref, s1_ref, p_ref, a_ref,
               sel_ref, m_ref, acc_ref, *, nb, groups):
    step = pl.program_id(2)
    blk = lax.broadcasted_iota(jnp.int32, (nb, GROUP_COLS), 0)
    own_blk = 2 * step + _query_sub_block(GROUP_COLS)
    valid = blk < own_blk
    blk_f = blk.astype(_F32)
    qcats = []
    for g in range(groups):
        qcat = _stacked_queries(q_ref, g)
        qcats.append(qcat)
        kmean = kmean_ref[0, :, g * LANES:(g + 1) * LANES].astype(_BF16)
        gate = jnp.dot(kmean, qcat, preferred_element_type=_F32)
        gate = jnp.where(valid, gate, -jnp.inf)
        sel = jnp.where(blk == own_blk, 1.0, 0.0)
        for _ in range(MOBA_TOPK):
            mx = jnp.max(gate, axis=0, keepdims=True)
            idx = jnp.min(jnp.where(gate == mx, blk_f, float(nb)), axis=0, keepdims=True)
            pick = (blk_f == idx) & valid
            sel = jnp.where(pick, 1.0, sel)
            gate = jnp.where(pick, -jnp.inf, gate)
        sel_ref[:, g * GROUP_COLS:(g + 1) * GROUP_COLS] = sel

    def active_fn(j):
        return sel_ref[pl.ds(j, 1), :] > 0.0

    _flash_blocks(step, qcats, k_ref, vt_ref, bias_ref, (s0_ref, s1_ref), p_ref, a_ref, m_ref,
                  acc_ref, active_fn, all_past_active=False, split_heads=True)

    for g in range(groups):
        acc = acc_ref[g]
        o = acc[:HEAD_DIM] / acc[HEAD_DIM:HEAD_DIM + 1]
        o_t = jnp.concatenate([o[:, :Q_BLOCK], o[:, Q_BLOCK:]], axis=0)
        o_ref[0, :, g * LANES:(g + 1) * LANES] = o_t.T.astype(o_ref.dtype)


def _moba_attention(qk, vt, kmean, bias, *, width, groups):
    B, S, _ = qk.shape
    nb = S // KEY_BLOCK
    gw = groups * LANES
    chunks = width // gw
    body = functools.partial(_moba_body, nb=nb, groups=groups)
    return pl.pallas_call(
        body,
        grid=(B, chunks, S // Q_BLOCK),
        in_specs=[
            pl.BlockSpec((1, Q_BLOCK, gw), lambda b, p, i: (b, i, p)),
            pl.BlockSpec((1, S, gw), lambda b, p, i: (b, 0, chunks + p)),
            pl.BlockSpec((1, nb, gw, KEY_BLOCK), lambda b, p, i: (b, 0, p, 0)),
            pl.BlockSpec((1, nb, gw), lambda b, p, i: (b, 0, p)),
            _const_spec(bias.shape),
        ],
        out_specs=pl.BlockSpec((1, Q_BLOCK, gw), lambda b, p, i: (b, i, p)),
        out_shape=jax.ShapeDtypeStruct((B, S, width), _BF16),
        scratch_shapes=_attn_scratch(nb, groups, HEAD_DIM + BF16_ROWS),
        compiler_params=pltpu.CompilerParams(
            dimension_semantics=("parallel", "parallel", "arbitrary"),
            vmem_limit_bytes=VMEM_LIMIT_BYTES),
        name="moba_attention",
    )(qk, qk, vt, kmean, bias)


def _diff_body(q_ref, k_ref, vt_ref, bias_ref, lq1_ref, lk1_ref, lq2_ref, lk2_ref, subln_ref, o_ref,
               s0_ref, s1_ref, p_ref, a_ref, m_ref, acc_ref, *, lambda_init, groups):
    step = pl.program_id(2)
    qcats = [_stacked_queries(q_ref, g) for g in range(groups)]
    own_blk = 2 * step + _query_sub_block(groups * GROUP_COLS)

    def active_fn(j):
        return j <= own_blk

    _flash_blocks(step, qcats, k_ref, vt_ref, bias_ref, (s0_ref, s1_ref), p_ref, a_ref, m_ref,
                  acc_ref, active_fn, all_past_active=True, split_heads=False)

    lam = (jnp.exp(jnp.sum(lq1_ref[...] * lk1_ref[...], axis=-1, keepdims=True))
           - jnp.exp(jnp.sum(lq2_ref[...] * lk2_ref[...], axis=-1, keepdims=True))
           + lambda_init)
    for g in range(groups):
        acc = acc_ref[g]
        o = acc[:LANES] / acc[LANES:LANES + 1]
        a_t = o[:, :Q_BLOCK] - lam * o[:, Q_BLOCK:]
        ms = jnp.mean(a_t * a_t, axis=0, keepdims=True)
        a = (a_t * lax.rsqrt(ms + RMS_EPS)).T
        out = (a * subln_ref[...]) * (1.0 - lambda_init)
        o_ref[0, :, g * LANES:(g + 1) * LANES] = out.astype(o_ref.dtype)


def _diff_attention(qk, vt, bias, lq1, lk1, lq2, lk2, subln, *, width, lambda_init, groups):
    B, S, _ = qk.shape
    nb = S // KEY_BLOCK
    gw = groups * LANES
    chunks = width // gw
    body = functools.partial(_diff_body, lambda_init=lambda_init, groups=groups)
    vec = lambda n: _const_spec((1, n))
    return pl.pallas_call(
        body,
        grid=(B, chunks, S // Q_BLOCK),
        in_specs=[
            pl.BlockSpec((1, Q_BLOCK, gw), lambda b, h, i: (b, i, 2 * chunks + h)),
            pl.BlockSpec((1, S, gw), lambda b, h, i: (b, 0, 3 * chunks + h)),
            pl.BlockSpec((1, nb, gw, KEY_BLOCK), lambda b, h, i: (b, 0, chunks + h, 0)),
            _const_spec(bias.shape),
            vec(HEAD_DIM), vec(HEAD_DIM), vec(HEAD_DIM), vec(HEAD_DIM), vec(LANES),
        ],
        out_specs=pl.BlockSpec((1, Q_BLOCK, gw), lambda b, h, i: (b, i, h)),
        out_shape=jax.ShapeDtypeStruct((B, S, width), _BF16),
        scratch_shapes=_attn_scratch(0, groups, LANES + BF16_ROWS),
        compiler_params=pltpu.CompilerParams(
            dimension_semantics=("parallel", "parallel", "arbitrary"),
            vmem_limit_bytes=VMEM_LIMIT_BYTES),
        name="diff_attention",
    )(qk, qk, vt, bias, lq1, lk1, lq2, lk2, subln)


def _out_ffn_body(x_ref, mo_ref, do_ref, wo_ref, gain_ref, wg_ref, wu_ref, wd_ref, y_ref, *, width):
    x = x_ref[...]
    att = (jnp.dot(mo_ref[...], wo_ref[0:width, :], preferred_element_type=_F32)
           + jnp.dot(do_ref[...], wo_ref[width:2 * width, :], preferred_element_type=_F32))
    x1 = x + att
    ms = jnp.mean(x1 * x1, axis=-1, keepdims=True)
    h = (x1 * lax.rsqrt(ms + RMS_EPS) * gain_ref[...]).astype(_BF16)
    g = jnp.dot(h, wg_ref[...], preferred_element_type=_F32)
    u = jnp.dot(h, wu_ref[...], preferred_element_type=_F32)
    a = (g * jax.nn.sigmoid(g) * u).astype(_BF16)
    y_ref[...] = x1 + jnp.dot(a, wd_ref[...], preferred_element_type=_F32)


def _out_ffn(x2, mo2, do2, wo, gain, wg, wu, wd, *, tm):
    T, D = x2.shape
    width = mo2.shape[1]
    F = wg.shape[1]
    body = functools.partial(_out_ffn_body, width=width)
    row = lambda n: pl.BlockSpec((tm, n), lambda t: (t, 0))
    return pl.pallas_call(
        body,
        grid=(T // tm,),
        in_specs=[row(D), row(width), row(width), _const_spec((2 * width, D)), _const_spec((1, D)),
                  _const_spec((D, F)), _const_spec((D, F)), _const_spec((F, D))],
        out_specs=row(D),
        out_shape=jax.ShapeDtypeStruct((T, D), _F32),
        compiler_params=pltpu.CompilerParams(
            dimension_semantics=("parallel",), vmem_limit_bytes=VMEM_LIMIT_BYTES),
        name="outproj_swiglu",
    )(x2, mo2, do2, wo, gain, wg, wu, wd)


def _qk_lane_order(a):
    half = HEAD_DIM // 2
    lead = a.shape[:-1]
    a = a.reshape(*lead, a.shape[-1] // LANES, 2, 2, half)
    return jnp.swapaxes(a, -3, -2).reshape(*lead, -1)


def _rope_tables(seq):
    inv = 1.0 / (ROPE_THETA ** (jnp.arange(0, HEAD_DIM, 2, dtype=_F32) / HEAD_DIM))
    ang = jnp.arange(seq, dtype=_F32)[:, None] * inv[None, :]
    cos = jnp.tile(jnp.cos(ang), (1, 2 * LANES // HEAD_DIM))
    sin = jnp.sin(ang)
    sin = jnp.concatenate([-sin, -sin, sin, sin], axis=-1)
    return cos, sin


def kernel(x, attn_norm, w_in, moba_q_norm, moba_k_norm, diff_q_norm, diff_k_norm,
           lambda_q1, lambda_k1, lambda_q2, lambda_k2, diff_subln, w_out,
           ffn_norm, w_gate, w_up, w_down):
    B, S, D = x.shape
    depth = w_in.shape[0]
    width = w_in.shape[2] // 6
    heads_per_group = width // HEAD_DIM
    assert S % Q_BLOCK == 0 and S // KEY_BLOCK >= Q_SUB, "sequence must be a multiple of 512"
    assert width % (ATTN_GROUPS * LANES) == 0 and w_out.shape[1] == 2 * width
    tm_in = Q_BLOCK
    tm_ffn = Q_BLOCK
    cos, sin = _rope_tables(S)
    bias = _causal_bias()
    gmat = jnp.kron(jnp.eye(MXU_COLS // HEAD_DIM, dtype=_F32),
                    jnp.full((HEAD_DIM, HEAD_DIM), 1.0 / HEAD_DIM, _F32))
    gmat = _qk_lane_order(_qk_lane_order(gmat).T).astype(_BF16)
    scale = HEAD_DIM ** -0.5 * math.log2(math.e)
    for l in range(depth):
        lambda_init = 0.8 - 0.6 * math.exp(-0.3 * l)
        tile = lambda g: _qk_lane_order(jnp.tile(g.astype(_F32), heads_per_group))
        hgain = jnp.stack([tile(moba_q_norm[l]) * scale, tile(moba_k_norm[l]),
                           tile(diff_q_norm[l]) * scale, tile(diff_k_norm[l])])
        w_groups = [w_in[l][:, i * width:(i + 1) * width] for i in range(6)]
        w_groups = [(w if i in (2, 5) else _qk_lane_order(w)).astype(_BF16)
                    for i, w in enumerate(w_groups)]
        qk, vt, kmean = _inproj(x, attn_norm[l][None, :], w_groups, gmat, hgain, cos, sin,
                                tm=tm_in)
        kmean = kmean.reshape(B, S // MOBA_BLOCK, width)
        moba_out = _moba_attention(qk, vt, kmean, bias, width=width, groups=ATTN_GROUPS)
        diff_out = _diff_attention(qk, vt, bias, lambda_q1[l][None, :], lambda_k1[l][None, :],
                                   lambda_q2[l][None, :], lambda_k2[l][None, :],
                                   diff_subln[l][None, :], width=width, lambda_init=lambda_init,
                                   groups=ATTN_GROUPS)
        y = _out_ffn(x.reshape(B * S, D), moba_out.reshape(B * S, width),
                     diff_out.reshape(B * S, width), w_out[l].astype(_BF16), ffn_norm[l][None, :],
                     w_gate[l].astype(_BF16), w_up[l].astype(_BF16), w_down[l].astype(_BF16),
                     tm=tm_ffn)
        x = y.reshape(B, S, D)
    return x
```

```python
import functools
import math

import jax
import jax.numpy as jnp
from jax import lax
from jax.experimental import pallas as pl
from jax.experimental.pallas import tpu as pltpu

HEAD_DIM = 64
LANES = 128
BF16_ROWS = 16
MXU_COLS = 256
MOBA_BLOCK = 256
MOBA_TOPK = 3
ROPE_THETA = 10000.0
RMS_EPS = 1e-6
KEY_BLOCK = MOBA_BLOCK
Q_SUB = 2
Q_BLOCK = Q_SUB * MOBA_BLOCK
ATTN_GROUPS = 4
GROUP_COLS = 2 * Q_BLOCK
NEG_INIT = float(jnp.finfo(jnp.float32).min)
VMEM_LIMIT_BYTES = 56 * 1024 * 1024

_F32 = jnp.float32
_BF16 = jnp.bfloat16


def _const_spec(shape):
    zeros = (0,) * len(shape)
    return pl.BlockSpec(shape, lambda *_: zeros, pipeline_mode=pl.Buffered(1))


def _inproj_body(x_ref, gain_ref, wmq_ref, wmk_ref, wmv_ref, wdq_ref, wdk_ref, wdv_ref, gmat_ref,
                 hgain_ref, cos_ref, sin_ref, qk_ref, vt_ref, kmean_ref, *, width):
    x = x_ref[0]
    ms = jnp.mean(x * x, axis=-1, keepdims=True)
    h = (x * lax.rsqrt(ms + RMS_EPS) * gain_ref[...]).astype(_BF16)

    def project(w_ref):
        return jnp.dot(h, w_ref[...], preferred_element_type=_F32)

    tm = x.shape[0]
    cos = cos_ref[...]
    sin = sin_ref[...]

    for gi, w_ref in enumerate((wmq_ref, wmk_ref, wdq_ref, wdk_ref)):
        p = project(w_ref)
        sq = (p * p).astype(_BF16)
        msq = jnp.concatenate(
            [jnp.dot(sq[:, c:c + MXU_COLS], gmat_ref[...], preferred_element_type=_F32)
             for c in range(0, width, MXU_COLS)], axis=1)
        y = p * lax.rsqrt(msq + RMS_EPS) * hgain_ref[gi:gi + 1, :]
        for cc in range(width // LANES):
            yc = y[:, cc * LANES:(cc + 1) * LANES]
            oc = yc * cos + pltpu.roll(yc, LANES // 2, 1) * sin
            qk_ref[0, :, gi * width + cc * LANES:gi * width + (cc + 1) * LANES] = oc.astype(_BF16)
            if gi == 1:
                for rb in range(tm // MOBA_BLOCK):
                    kmean_ref[0, 0, rb:rb + 1, cc * LANES:(cc + 1) * LANES] = jnp.mean(
                        oc[rb * MOBA_BLOCK:(rb + 1) * MOBA_BLOCK], axis=0, keepdims=True)

    for vi, w_ref in enumerate((wmv_ref, wdv_ref)):
        v = project(w_ref)
        for rb in range(tm // KEY_BLOCK):
            vt_ref[0, rb, vi * width:(vi + 1) * width, :] = (
                v[rb * KEY_BLOCK:(rb + 1) * KEY_BLOCK].T.astype(_BF16))


def _inproj(x, gain, w_groups, gmat, hgain, cos, sin, *, tm):
    B, S, D = x.shape
    width = w_groups[0].shape[1]
    body = functools.partial(_inproj_body, width=width)
    return pl.pallas_call(
        body,
        grid=(B, S // tm),
        in_specs=[
            pl.BlockSpec((1, tm, D), lambda b, t: (b, t, 0)),
            _const_spec((1, D)),
            *[_const_spec((D, width)) for _ in w_groups],
            _const_spec((MXU_COLS, MXU_COLS)),
            _const_spec((4, width)),
            pl.BlockSpec((tm, LANES), lambda b, t: (t, 0)),
            pl.BlockSpec((tm, LANES), lambda b, t: (t, 0)),
        ],
        out_specs=[
            pl.BlockSpec((1, tm, 4 * width), lambda b, t: (b, t, 0)),
            pl.BlockSpec((1, tm // KEY_BLOCK, 2 * width, KEY_BLOCK), lambda b, t: (b, t, 0, 0)),
            pl.BlockSpec((1, 1, tm // MOBA_BLOCK, width), lambda b, t: (b, t, 0, 0)),
        ],
        out_shape=[
            jax.ShapeDtypeStruct((B, S, 4 * width), _BF16),
            jax.ShapeDtypeStruct((B, S // KEY_BLOCK, 2 * width, KEY_BLOCK), _BF16),
            jax.ShapeDtypeStruct((B, S // tm, tm // MOBA_BLOCK, width), _F32),
        ],
        compiler_params=pltpu.CompilerParams(
            dimension_semantics=("parallel", "parallel"), vmem_limit_bytes=VMEM_LIMIT_BYTES),
        name="inproj_qknorm_rope",
    )(x, gain, *w_groups, gmat, hgain, cos, sin)


def _stacked_queries(q_ref, g):
    q = q_ref[0, :, g * LANES:(g + 1) * LANES]
    lane = lax.broadcasted_iota(jnp.int32, q.shape, 1)
    zero = jnp.zeros_like(q)
    first = (lane // (HEAD_DIM // 2)) % 2 == 0
    stacked = jnp.concatenate([jnp.where(first, q, zero), jnp.where(first, zero, q)], axis=0)
    return stacked.astype(_F32).T.astype(_BF16)


def _query_sub_block(cols):
    col = lax.broadcasted_iota(jnp.int32, (1, cols), 1)
    return (col // MOBA_BLOCK) % Q_SUB


def _causal_bias():
    kpos = lax.broadcasted_iota(jnp.int32, (KEY_BLOCK, GROUP_COLS), 0)
    col = lax.broadcasted_iota(jnp.int32, (KEY_BLOCK, GROUP_COLS), 1)
    causal = kpos <= (col % MOBA_BLOCK)
    second = ((col // MOBA_BLOCK) % Q_SUB) == 1
    neg = jnp.float32(-jnp.inf)
    tiles = [jnp.where(causal | second, 0.0, neg), jnp.where(causal, 0.0, neg)]
    return jnp.stack(tiles).astype(_BF16)


def _flash_blocks(step, qcats, k_ref, vt_ref, bias_ref, s_refs, p_ref, a_ref, m_ref, acc_ref,
                  active_fn, all_past_active, split_heads):
    groups = len(qcats)
    n_chunks = KEY_BLOCK // BF16_ROWS
    ones = jnp.ones((BF16_ROWS, KEY_BLOCK), _BF16)
    own = 2 * step

    group_ranges = [(g * GROUP_COLS, GROUP_COLS) for g in range(groups)]
    second_ranges = [(g * GROUP_COLS + h * Q_BLOCK + MOBA_BLOCK, MOBA_BLOCK)
                     for g in range(groups) for h in range(2)]

    def scores(j, slot, second_only=False):
        kb = k_ref[0, pl.ds(pl.multiple_of(j * KEY_BLOCK, KEY_BLOCK), KEY_BLOCK), :]
        for c0, n in (second_ranges if second_only else group_ranges):
            g, off = divmod(c0, GROUP_COLS)
            s_refs[slot][:, c0:c0 + n] = jnp.dot(
                kb[:, g * LANES:(g + 1) * LANES], qcats[g][:, off:off + n],
                preferred_element_type=_F32).astype(_BF16)

    def softmax(j, slot, bias_tile=None, second_only=False):
        s_ref = s_refs[slot]
        ranges = second_ranges if second_only else group_ranges
        cols = sum(n for _, n in ranges)

        def take(row_ref):
            if not second_only:
                return row_ref[...]
            return jnp.concatenate([row_ref[:, c0:c0 + n] for c0, n in ranges], axis=1)

        def put(dst, value):
            if not second_only:
                dst(slice(None), value)
                return
            at = 0
            for c0, n in ranges:
                dst(slice(c0, c0 + n), value[:, at:at + n])
                at += n

        def rows(c):
            r = slice(c * BF16_ROWS, (c + 1) * BF16_ROWS)
            if bias_tile is None:
                return s_ref[r, :]
            return jnp.concatenate(
                [s_ref[r, c0:c0 + n]
                 + bias_ref[bias_tile, r, c0 % GROUP_COLS:c0 % GROUP_COLS + n] for c0, n in ranges],
                axis=1)

        mx = rows(0)
        for c in range(1, n_chunks):
            mx = jnp.maximum(mx, rows(c))
        bmax = jnp.max(mx.astype(_F32), axis=0, keepdims=True)
        m_old = take(m_ref)
        if bias_tile is None and all_past_active:
            m_new = jnp.maximum(m_old, bmax)
            m_used = m_new
        else:
            active = active_fn(j)
            if second_only:
                active = jnp.concatenate([active[:, c0:c0 + n] for c0, n in ranges], axis=1)
            m_new = jnp.where(active, jnp.maximum(m_old, bmax), m_old)
            m_used = jnp.where(active, m_new, jnp.inf)
        put(lambda cs, v: m_ref.__setitem__((slice(None), cs), v), m_new)
        put(lambda cs, v: a_ref.__setitem__((slot, slice(None), cs), v), jnp.exp2(m_old - m_new))
        m_tile = jnp.broadcast_to(m_used, (BF16_ROWS, cols)).astype(_BF16)
        for c in range(n_chunks):
            r = slice(c * BF16_ROWS, (c + 1) * BF16_ROWS)
            put(lambda cs, v: p_ref.__setitem__((slot, r, cs), v), jnp.exp2(rows(c) - m_tile))

    def values(j, slot, second_only=False):
        alpha = a_ref[slot]
        first = MOBA_BLOCK if second_only else 0
        for g in range(groups):
            vt = vt_ref[0, j, g * LANES:(g + 1) * LANES, :]
            for h in range(2):
                if split_heads:
                    lhs = jnp.concatenate([vt[h * HEAD_DIM:(h + 1) * HEAD_DIM], ones], axis=0)
                    span = (h * Q_BLOCK + first, (h + 1) * Q_BLOCK)
                elif second_only:
                    lhs = jnp.concatenate([vt, ones], axis=0)
                    span = (h * Q_BLOCK + first, (h + 1) * Q_BLOCK)
                elif h == 0:
                    lhs = jnp.concatenate([vt, ones], axis=0)
                    span = (0, GROUP_COLS)
                else:
                    continue
                cols = slice(g * GROUP_COLS + span[0], g * GROUP_COLS + span[1])
                acc_ref[g, :, span[0]:span[1]] = (
                    alpha[:, cols] * acc_ref[g, :, span[0]:span[1]]
                    + jnp.dot(lhs, p_ref[slot, :, cols], preferred_element_type=_F32))

    m_ref[...] = jnp.full(m_ref.shape, NEG_INIT, _F32)
    acc_ref[...] = jnp.zeros(acc_ref.shape, _F32)
    a_ref[1] = jnp.ones(a_ref.shape[1:], _F32)
    p_ref[1] = jnp.zeros(p_ref.shape[1:], _BF16)
    scores(0, 0)

    def past_pair(t, carry):
        j = 2 * t
        scores(j + 1, 1)
        values(jnp.maximum(j - 1, 0), 1)
        softmax(j, 0)
        scores(j + 2, 0)
        values(j, 0)
        softmax(j + 1, 1)
        return carry

    lax.fori_loop(0, step, past_pair, 0)
    scores(own + 1, 1, second_only=True)
    values(jnp.maximum(own - 1, 0), 1)
    softmax(own, 0, bias_tile=0)
    values(own, 0)
    softmax(own + 1, 1, bias_tile=1, second_only=True)
    values(own + 1, 1, second_only=True)


def _attn_scratch(nb_sel, groups, acc_rows):
    cols = groups * GROUP_COLS
    shapes = [pltpu.VMEM((KEY_BLOCK, cols), _BF16), pltpu.VMEM((KEY_BLOCK, cols), _BF16),
              pltpu.VMEM((2, KEY_BLOCK, cols), _BF16),
              pltpu.VMEM((2, 1, cols), _F32)]
    if nb_sel:
        shapes.append(pltpu.VMEM((nb_sel, cols), _F32))
    shapes += [pltpu.VMEM((1, cols), _F32),
               pltpu.VMEM((groups, acc_rows, GROUP_COLS), _F32)]
    return shapes


def _moba_body(q_ref, k_ref, vt_ref, kmean_ref, bias_ref, o_ref, s0_ref, s1_ref, p_ref, a_ref,
               sel_ref, m_ref, acc_ref, *, nb, groups):
    step = pl.program_id(2)
    blk = lax.broadcasted_iota(jnp.int32, (nb, GROUP_COLS), 0)
    own_blk = 2 * step + _query_sub_block(GROUP_COLS)
    valid = blk < own_blk
    blk_f = blk.astype(_F32)
    qcats = []
    for g in range(groups):
        qcat = _stacked_queries(q_ref, g)
        qcats.append(qcat)
        kmean = kmean_ref[0, :, g * LANES:(g + 1) * LANES].astype(_BF16)
        gate = jnp.dot(kmean, qcat, preferred_element_type=_F32)
        gate = jnp.where(valid, gate, -jnp.inf)
        sel = jnp.where(blk == own_blk, 1.0, 0.0)
        for _ in range(MOBA_TOPK):
            mx = jnp.max(gate, axis=0, keepdims=True)
            idx = jnp.min(jnp.where(gate == mx, blk_f, float(nb)), axis=0, keepdims=True)
            pick = (blk_f == idx) & valid
            sel = jnp.where(pick, 1.0, sel)
            gate = jnp.where(pick, -jnp.inf, gate)
        sel_ref[:, g * GROUP_COLS:(g + 1) * GROUP_COLS] = sel

    def active_fn(j):
        return sel_ref[pl.ds(j, 1), :] > 0.0

    _flash_blocks(step, qcats, k_ref, vt_ref, bias_ref, (s0_ref, s1_ref), p_ref, a_ref, m_ref,
                  acc_ref, active_fn, all_past_active=False, split_heads=True)

    for g in range(groups):
        acc = acc_ref[g]
        o = acc[:HEAD_DIM] / acc[HEAD_DIM:HEAD_DIM + 1]
        o_t = jnp.concatenate([o[:, :Q_BLOCK], o[:, Q_BLOCK:]], axis=0)
        o_ref[0, :, g * LANES:(g + 1) * LANES] = o_t.T.astype(o_ref.dtype)


def _moba_attention(qk, vt, kmean, bias, *, width, groups):
    B, S, _ = qk.shape
    nb = S // KEY_BLOCK
    gw = groups * LANES
    chunks = width // gw
    body = functools.partial(_moba_body, nb=nb, groups=groups)
    return pl.pallas_call(
        body,
        grid=(B, chunks, S // Q_BLOCK),
        in_specs=[
            pl.BlockSpec((1, Q_BLOCK, gw), lambda b, p, i: (b, i, p)),
            pl.BlockSpec((1, S, gw), lambda b, p, i: (b, 0, chunks + p)),
            pl.BlockSpec((1, nb, gw, KEY_BLOCK), lambda b, p, i: (b, 0, p, 0)),
            pl.BlockSpec((1, nb, gw), lambda b, p, i: (b, 0, p)),
            _const_spec(bias.shape),
        ],
        out_specs=pl.BlockSpec((1, Q_BLOCK, gw), lambda b, p, i: (b, i, p)),
        out_shape=jax.ShapeDtypeStruct((B, S, width), _BF16),
        scratch_shapes=_attn_scratch(nb, groups, HEAD_DIM + BF16_ROWS),
        compiler_params=pltpu.CompilerParams(
            dimension_semantics=("parallel", "parallel", "arbitrary"),
            vmem_limit_bytes=VMEM_LIMIT_BYTES),
        name="moba_attention",
    )(qk, qk, vt, kmean, bias)


def _diff_body(q_ref, k_ref, vt_ref, bias_ref, lq1_ref, lk1_ref, lq2_ref, lk2_ref, subln_ref, o_ref,
               s0_ref, s1_ref, p_ref, a_ref, m_ref, acc_ref, *, lambda_init, groups):
    step = pl.program_id(2)
    qcats = [_stacked_queries(q_ref, g) for g in range(groups)]
    own_blk = 2 * step + _query_sub_block(groups * GROUP_COLS)

    def active_fn(j):
        return j <= own_blk

    _flash_blocks(step, qcats, k_ref, vt_ref, bias_ref, (s0_ref, s1_ref), p_ref, a_ref, m_ref,
                  acc_ref, active_fn, all_past_active=True, split_heads=False)

    lam = (jnp.exp(jnp.sum(lq1_ref[...] * lk1_ref[...], axis=-1, keepdims=True))
           - jnp.exp(jnp.sum(lq2_ref[...] * lk2_ref[...], axis=-1, keepdims=True))
           + lambda_init)
    for g in range(groups):
        acc = acc_ref[g]
        o = acc[:LANES] / acc[LANES:LANES + 1]
        a_t = o[:, :Q_BLOCK] - lam * o[:, Q_BLOCK:]
        ms = jnp.mean(a_t * a_t, axis=0, keepdims=True)
        a = (a_t * lax.rsqrt(ms + RMS_EPS)).T
        out = (a * subln_ref[...]) * (1.0 - lambda_init)
        o_ref[0, :, g * LANES:(g + 1) * LANES] = out.astype(o_ref.dtype)


def _diff_attention(qk, vt, bias, lq1, lk1, lq2, lk2, subln, *, width, lambda_init, groups):
    B, S, _ = qk.shape
    nb = S // KEY_BLOCK
    gw = groups * LANES
    chunks = width // gw
    body = functools.partial(_diff_body, lambda_init=lambda_init, groups=groups)
    vec = lambda n: _const_spec((1, n))
    return pl.pallas_call(
        body,
        grid=(B, chunks, S // Q_BLOCK),
        in_specs=[
            pl.BlockSpec((1, Q_BLOCK, gw), lambda b, h, i: (b, i, 2 * chunks + h)),
            pl.BlockSpec((1, S, gw), lambda b, h, i: (b, 0, 3 * chunks + h)),
            pl.BlockSpec((1, nb, gw, KEY_BLOCK), lambda b, h, i: (b, 0, chunks + h, 0)),
            _const_spec(bias.shape),
            vec(HEAD_DIM), vec(HEAD_DIM), vec(HEAD_DIM), vec(HEAD_DIM), vec(LANES),
        ],
        out_specs=pl.BlockSpec((1, Q_BLOCK, gw), lambda b, h, i: (b, i, h)),
        out_shape=jax.ShapeDtypeStruct((B, S, width), _BF16),
        scratch_shapes=_attn_scratch(0, groups, LANES + BF16_ROWS),
        compiler_params=pltpu.CompilerParams(
            dimension_semantics=("parallel", "parallel", "arbitrary"),
            vmem_limit_bytes=VMEM_LIMIT_BYTES),
        name="diff_attention",
    )(qk, qk, vt, bias, lq1, lk1, lq2, lk2, subln)


def _out_ffn_body(x_ref, mo_ref, do_ref, wo_ref, gain_ref, wg_ref, wu_ref, wd_ref, y_ref, *, width):
    x = x_ref[...]
    att = (jnp.dot(mo_ref[...], wo_ref[0:width, :], preferred_element_type=_F32)
           + jnp.dot(do_ref[...], wo_ref[width:2 * width, :], preferred_element_type=_F32))
    x1 = x + att
    ms = jnp.mean(x1 * x1, axis=-1, keepdims=True)
    h = (x1 * lax.rsqrt(ms + RMS_EPS) * gain_ref[...]).astype(_BF16)
    g = jnp.dot(h, wg_ref[...], preferred_element_type=_F32)
    u = jnp.dot(h, wu_ref[...], preferred_element_type=_F32)
    a = (g * jax.nn.sigmoid(g) * u).astype(_BF16)
    y_ref[...] = x1 + jnp.dot(a, wd_ref[...], preferred_element_type=_F32)


def _out_ffn(x2, mo2, do2, wo, gain, wg, wu, wd, *, tm):
    T, D = x2.shape
    width = mo2.shape[1]
    F = wg.shape[1]
    body = functools.partial(_out_ffn_body, width=width)
    row = lambda n: pl.BlockSpec((tm, n), lambda t: (t, 0))
    return pl.pallas_call(
        body,
        grid=(T // tm,),
        in_specs=[row(D), row(width), row(width), _const_spec((2 * width, D)), _const_spec((1, D)),
                  _const_spec((D, F)), _const_spec((D, F)), _const_spec((F, D))],
        out_specs=row(D),
        out_shape=jax.ShapeDtypeStruct((T, D), _F32),
        compiler_params=pltpu.CompilerParams(
            dimension_semantics=("parallel",), vmem_limit_bytes=VMEM_LIMIT_BYTES),
        name="outproj_swiglu",
    )(x2, mo2, do2, wo, gain, wg, wu, wd)


def _qk_lane_order(a):
    half = HEAD_DIM // 2
    lead = a.shape[:-1]
    a = a.reshape(*lead, a.shape[-1] // LANES, 2, 2, half)
    return jnp.swapaxes(a, -3, -2).reshape(*lead, -1)


def _rope_tables(seq):
    inv = 1.0 / (ROPE_THETA ** (jnp.arange(0, HEAD_DIM, 2, dtype=_F32) / HEAD_DIM))
    ang = jnp.arange(seq, dtype=_F32)[:, None] * inv[None, :]
    cos = jnp.tile(jnp.cos(ang), (1, 2 * LANES // HEAD_DIM))
    sin = jnp.sin(ang)
    sin = jnp.concatenate([-sin, -sin, sin, sin], axis=-1)
    return cos, sin


def kernel(x, attn_norm, w_in, moba_q_norm, moba_k_norm, diff_q_norm, diff_k_norm,
           lambda_q1, lambda_k1, lambda_q2, lambda_k2, diff_subln, w_out,
           ffn_norm, w_gate, w_up, w_down):
    B, S, D = x.shape
    depth = w_in.shape[0]
    width = w_in.shape[2] // 6
    heads_per_group = width // HEAD_DIM
    assert S % Q_BLOCK == 0 and S // KEY_BLOCK >= Q_SUB, "sequence must be a multiple of 512"
    assert width % (ATTN_GROUPS * LANES) == 0 and w_out.shape[1] == 2 * width
    tm_in = Q_BLOCK
    tm_ffn = Q_BLOCK
    cos, sin = _rope_tables(S)
    bias = _causal_bias()
    gmat = jnp.kron(jnp.eye(MXU_COLS // HEAD_DIM, dtype=_F32),
                    jnp.full((HEAD_DIM, HEAD_DIM), 1.0 / HEAD_DIM, _F32))
    gmat = _qk_lane_order(_qk_lane_order(gmat).T).astype(_BF16)
    scale = HEAD_DIM ** -0.5 * math.log2(math.e)
    for l in range(depth):
        lambda_init = 0.8 - 0.6 * math.exp(-0.3 * l)
        tile = lambda g: _qk_lane_order(jnp.tile(g.astype(_F32), heads_per_group))
        hgain = jnp.stack([tile(moba_q_norm[l]) * scale, tile(moba_k_norm[l]),
                           tile(diff_q_norm[l]) * scale, tile(diff_k_norm[l])])
        w_groups = [w_in[l][:, i * width:(i + 1) * width] for i in range(6)]
        w_groups = [(w if i in (2, 5) else _qk_lane_order(w)).astype(_BF16)
                    for i, w in enumerate(w_groups)]
        qk, vt, kmean = _inproj(x, attn_norm[l][None, :], w_groups, gmat, hgain, cos, sin,
                                tm=tm_in)
        kmean = kmean.reshape(B, S // MOBA_BLOCK, width)
        moba_out = _moba_attention(qk, vt, kmean, bias, width=width, groups=ATTN_GROUPS)
        diff_out = _diff_attention(qk, vt, bias, lambda_q1[l][None, :], lambda_k1[l][None, :],
                                   lambda_q2[l][None, :], lambda_k2[l][None, :],
                                   diff_subln[l][None, :], width=width, lambda_init=lambda_init,
                                   groups=ATTN_GROUPS)
        y = _out_ffn(x.reshape(B * S, D), moba_out.reshape(B * S, width),
                     diff_out.reshape(B * S, width), w_out[l].astype(_BF16), ffn_norm[l][None, :],
                     w_gate[l].astype(_BF16), w_up[l].astype(_BF16), w_down[l].astype(_BF16),
                     tm=tm_ffn)
        x = y.reshape(B, S, D)
    return x
```

```python
import functools
import math

import jax
import jax.numpy as jnp
from jax import lax
from jax.experimental import pallas as pl
from jax.experimental.pallas import tpu as pltpu

HEAD_DIM = 64
LANES = 128
BF16_ROWS = 16
MXU_COLS = 256
MOBA_BLOCK = 256
MOBA_TOPK = 3
ROPE_THETA = 10000.0
RMS_EPS = 1e-6
KEY_BLOCK = MOBA_BLOCK
Q_SUB = 2
Q_BLOCK = Q_SUB * MOBA_BLOCK
ATTN_GROUPS = 4
GROUP_COLS = 2 * Q_BLOCK
NEG_INIT = float(jnp.finfo(jnp.float32).min)
VMEM_LIMIT_BYTES = 56 * 1024 * 1024

_F32 = jnp.float32
_BF16 = jnp.bfloat16


def _const_spec(shape):
    zeros = (0,) * len(shape)
    return pl.BlockSpec(shape, lambda *_: zeros, pipeline_mode=pl.Buffered(1))


def _inproj_body(x_ref, gain_ref, wmq_ref, wmk_ref, wmv_ref, wdq_ref, wdk_ref, wdv_ref, gmat_ref,
                 hgain_ref, cos_ref, sin_ref, qk_ref, vt_ref, kmean_ref, *, width):
    x = x_ref[0]
    ms = jnp.mean(x * x, axis=-1, keepdims=True)
    h = (x * lax.rsqrt(ms + RMS_EPS) * gain_ref[...]).astype(_BF16)

    def project(w_ref):
        return jnp.dot(h, w_ref[...], preferred_element_type=_F32)

    tm = x.shape[0]
    cos = cos_ref[...]
    sin = sin_ref[...]

    for gi, w_ref in enumerate((wmq_ref, wmk_ref, wdq_ref, wdk_ref)):
        p = project(w_ref)
        sq = (p * p).astype(_BF16)
        msq = jnp.concatenate(
            [jnp.dot(sq[:, c:c + MXU_COLS], gmat_ref[...], preferred_element_type=_F32)
             for c in range(0, width, MXU_COLS)], axis=1)
        y = p * lax.rsqrt(msq + RMS_EPS) * hgain_ref[gi:gi + 1, :]
        for cc in range(width // LANES):
            yc = y[:, cc * LANES:(cc + 1) * LANES]
            oc = yc * cos + pltpu.roll(yc, LANES // 2, 1) * sin
            qk_ref[0, :, gi * width + cc * LANES:gi * width + (cc + 1) * LANES] = oc.astype(_BF16)
            if gi == 1:
                for rb in range(tm // MOBA_BLOCK):
                    kmean_ref[0, 0, rb:rb + 1, cc * LANES:(cc + 1) * LANES] = jnp.mean(
                        oc[rb * MOBA_BLOCK:(rb + 1) * MOBA_BLOCK], axis=0, keepdims=True)

    for vi, w_ref in enumerate((wmv_ref, wdv_ref)):
        v = project(w_ref)
        for rb in range(tm // KEY_BLOCK):
            vt_ref[0, rb, vi * width:(vi + 1) * width, :] = (
                v[rb * KEY_BLOCK:(rb + 1) * KEY_BLOCK].T.astype(_BF16))


def _inproj(x, gain, w_groups, gmat, hgain, cos, sin, *, tm):
    B, S, D = x.shape
    width = w_groups[0].shape[1]
    body = functools.partial(_inproj_body, width=width)
    return pl.pallas_call(
        body,
        grid=(B, S // tm),
        in_specs=[
            pl.BlockSpec((1, tm, D), lambda b, t: (b, t, 0)),
            _const_spec((1, D)),
            *[_const_spec((D, width)) for _ in w_groups],
            _const_spec((MXU_COLS, MXU_COLS)),
            _const_spec((4, width)),
            pl.BlockSpec((tm, LANES), lambda b, t: (t, 0)),
            pl.BlockSpec((tm, LANES), lambda b, t: (t, 0)),
        ],
        out_specs=[
            pl.BlockSpec((1, tm, 4 * width), lambda b, t: (b, t, 0)),
            pl.BlockSpec((1, tm // KEY_BLOCK, 2 * width, KEY_BLOCK), lambda b, t: (b, t, 0, 0)),
            pl.BlockSpec((1, 1, tm // MOBA_BLOCK, width), lambda b, t: (b, t, 0, 0)),
        ],
        out_shape=[
            jax.ShapeDtypeStruct((B, S, 4 * width), _BF16),
            jax.ShapeDtypeStruct((B, S // KEY_BLOCK, 2 * width, KEY_BLOCK), _BF16),
            jax.ShapeDtypeStruct((B, S // tm, tm // MOBA_BLOCK, width), _F32),
        ],
        compiler_params=pltpu.CompilerParams(
            dimension_semantics=("parallel", "parallel"), vmem_limit_bytes=VMEM_LIMIT_BYTES),
        name="inproj_qknorm_rope",
    )(x, gain, *w_groups, gmat, hgain, cos, sin)


def _stacked_queries(q_ref, g):
    q = q_ref[0, :, g * LANES:(g + 1) * LANES]
    lane = lax.broadcasted_iota(jnp.int32, q.shape, 1)
    zero = jnp.zeros_like(q)
    first = (lane // (HEAD_DIM // 2)) % 2 == 0
    stacked = jnp.concatenate([jnp.where(first, q, zero), jnp.where(first, zero, q)], axis=0)
    return stacked.astype(_F32).T.astype(_BF16)


def _query_sub_block(cols):
    col = lax.broadcasted_iota(jnp.int32, (1, cols), 1)
    return (col // MOBA_BLOCK) % Q_SUB


def _causal_bias():
    kpos = lax.broadcasted_iota(jnp.int32, (KEY_BLOCK, GROUP_COLS), 0)
    col = lax.broadcasted_iota(jnp.int32, (KEY_BLOCK, GROUP_COLS), 1)
    causal = kpos <= (col % MOBA_BLOCK)
    second = ((col // MOBA_BLOCK) % Q_SUB) == 1
    neg = jnp.float32(-jnp.inf)
    tiles = [jnp.where(causal | second, 0.0, neg), jnp.where(causal, 0.0, neg)]
    return jnp.stack(tiles).astype(_BF16)


def _flash_blocks(step, qcats, k_ref, vt_ref, bias_ref, s_refs, p_ref, a_ref, m_ref, acc_ref, qt_ref,
                  active_fn, all_past_active, split_heads):
    groups = len(qcats)
    n_chunks = KEY_BLOCK // BF16_ROWS
    ones = jnp.ones((BF16_ROWS, KEY_BLOCK), _BF16)
    own = 2 * step

    group_ranges = [(g * GROUP_COLS, GROUP_COLS) for g in range(groups)]
    second_ranges = [(g * GROUP_COLS + h * Q_BLOCK + MOBA_BLOCK, MOBA_BLOCK)
                     for g in range(groups) for h in range(2)]

    def scores(j, slot, second_only=False):
        kb = k_ref[0, pl.ds(pl.multiple_of(j * KEY_BLOCK, KEY_BLOCK), KEY_BLOCK), :]
        for c0, n in (second_ranges if second_only else group_ranges):
            g, off = divmod(c0, GROUP_COLS)
            s_refs[slot][:, c0:c0 + n] = jnp.dot(
                kb[:, g * LANES:(g + 1) * LANES], qt_ref[g, :, off:off + n],
                preferred_element_type=_F32).astype(_BF16)

    def softmax(j, slot, bias_tile=None, second_only=False):
        s_ref = s_refs[slot]
        ranges = second_ranges if second_only else group_ranges
        cols = sum(n for _, n in ranges)

        def take(row_ref):
            if not second_only:
                return row_ref[...]
            return jnp.concatenate([row_ref[:, c0:c0 + n] for c0, n in ranges], axis=1)

        def put(dst, value):
            if not second_only:
                dst(slice(None), value)
                return
            at = 0
            for c0, n in ranges:
                dst(slice(c0, c0 + n), value[:, at:at + n])
                at += n

        def rows(c):
            r = slice(c * BF16_ROWS, (c + 1) * BF16_ROWS)
            if bias_tile is None:
                return s_ref[r, :]
            return jnp.concatenate(
                [s_ref[r, c0:c0 + n]
                 + bias_ref[bias_tile, r, c0 % GROUP_COLS:c0 % GROUP_COLS + n] for c0, n in ranges],
                axis=1)

        mx = rows(0)
        for c in range(1, n_chunks):
            mx = jnp.maximum(mx, rows(c))
        bmax = jnp.max(mx.astype(_F32), axis=0, keepdims=True)
        m_old = take(m_ref)
        if bias_tile is None and all_past_active:
            m_new = jnp.maximum(m_old, bmax)
            m_used = m_new
        else:
            active = active_fn(j)
            if second_only:
                active = jnp.concatenate([active[:, c0:c0 + n] for c0, n in ranges], axis=1)
            m_new = jnp.where(active, jnp.maximum(m_old, bmax), m_old)
            m_used = jnp.where(active, m_new, jnp.inf)
        put(lambda cs, v: m_ref.__setitem__((slice(None), cs), v), m_new)
        put(lambda cs, v: a_ref.__setitem__((slot, slice(None), cs), v), jnp.exp2(m_old - m_new))
        m_tile = jnp.broadcast_to(m_used, (BF16_ROWS, cols)).astype(_BF16)
        for c in range(n_chunks):
            r = slice(c * BF16_ROWS, (c + 1) * BF16_ROWS)
            put(lambda cs, v: p_ref.__setitem__((slot, r, cs), v), jnp.exp2(rows(c) - m_tile))

    def values(j, slot, second_only=False):
        alpha = a_ref[slot]
        first = MOBA_BLOCK if second_only else 0
        for g in range(groups):
            vt = vt_ref[0, j, g * LANES:(g + 1) * LANES, :]
            for h in range(2):
                if split_heads:
                    lhs = jnp.concatenate([vt[h * HEAD_DIM:(h + 1) * HEAD_DIM], ones], axis=0)
                    span = (h * Q_BLOCK + first, (h + 1) * Q_BLOCK)
                elif second_only:
                    lhs = jnp.concatenate([vt, ones], axis=0)
                    span = (h * Q_BLOCK + first, (h + 1) * Q_BLOCK)
                elif h == 0:
                    lhs = jnp.concatenate([vt, ones], axis=0)
                    span = (0, GROUP_COLS)
                else:
                    continue
                cols = slice(g * GROUP_COLS + span[0], g * GROUP_COLS + span[1])
                acc_ref[g, :, span[0]:span[1]] = (
                    alpha[:, cols] * acc_ref[g, :, span[0]:span[1]]
                    + jnp.dot(lhs, p_ref[slot, :, cols], preferred_element_type=_F32))

    for g in range(groups):
        qt_ref[g] = qcats[g]
    m_ref[...] = jnp.full(m_ref.shape, NEG_INIT, _F32)
    acc_ref[...] = jnp.zeros(acc_ref.shape, _F32)
    a_ref[1] = jnp.ones(a_ref.shape[1:], _F32)
    p_ref[1] = jnp.zeros(p_ref.shape[1:], _BF16)
    scores(0, 0)

    def past_pair(t, carry):
        j = 2 * t
        scores(j + 1, 1)
        values(jnp.maximum(j - 1, 0), 1)
        softmax(j, 0)
        scores(j + 2, 0)
        values(j, 0)
        softmax(j + 1, 1)
        return carry

    lax.fori_loop(0, step, past_pair, 0)
    scores(own + 1, 1, second_only=True)
    values(jnp.maximum(own - 1, 0), 1)
    softmax(own, 0, bias_tile=0)
    values(own, 0)
    softmax(own + 1, 1, bias_tile=1, second_only=True)
    values(own + 1, 1, second_only=True)


def _attn_scratch(nb_sel, groups, acc_rows):
    cols = groups * GROUP_COLS
    shapes = [pltpu.VMEM((KEY_BLOCK, cols), _BF16), pltpu.VMEM((KEY_BLOCK, cols), _BF16),
              pltpu.VMEM((2, KEY_BLOCK, cols), _BF16),
              pltpu.VMEM((2, 1, cols), _F32)]
    if nb_sel:
        shapes.append(pltpu.VMEM((nb_sel, cols), _F32))
    shapes += [pltpu.VMEM((1, cols), _F32),
               pltpu.VMEM((groups, acc_rows, GROUP_COLS), _F32),
               pltpu.VMEM((groups, LANES, GROUP_COLS), _BF16)]
    return shapes


def _moba_body(q_ref, k_ref, vt_ref, kmean_ref, bias_ref, o_ref, s0_ref, s1_ref, p_ref, a_ref,
               sel_ref, m_ref, acc_ref, qt_ref, *, nb, groups):
    step = pl.program_id(2)
    blk = lax.broadcasted_iota(jnp.int32, (nb, GROUP_COLS), 0)
    own_blk = 2 * step + _query_sub_block(GROUP_COLS)
    valid = blk < own_blk
    blk_f = blk.astype(_F32)
    qcats = []
    for g in range(groups):
        qcat = _stacked_queries(q_ref, g)
        qcats.append(qcat)
        kmean = kmean_ref[0, :, g * LANES:(g + 1) * LANES].astype(_BF16)
        gate = jnp.dot(kmean, qcat, preferred_element_type=_F32)
        gate = jnp.where(valid, gate, -jnp.inf)
        sel = jnp.where(blk == own_blk, 1.0, 0.0)
        for _ in range(MOBA_TOPK):
            mx = jnp.max(gate, axis=0, keepdims=True)
            idx = jnp.min(jnp.where(gate == mx, blk_f, float(nb)), axis=0, keepdims=True)
            pick = (blk_f == idx) & valid
            sel = jnp.where(pick, 1.0, sel)
            gate = jnp.where(pick, -jnp.inf, gate)
        sel_ref[:, g * GROUP_COLS:(g + 1) * GROUP_COLS] = sel

    def active_fn(j):
        return sel_ref[pl.ds(j, 1), :] > 0.0

    _flash_blocks(step, qcats, k_ref, vt_ref, bias_ref, (s0_ref, s1_ref), p_ref, a_ref, m_ref,
                  acc_ref, qt_ref, active_fn, all_past_active=False, split_heads=True)

    for g in range(groups):
        acc = acc_ref[g]
        o = acc[:HEAD_DIM] / acc[HEAD_DIM:HEAD_DIM + 1]
        o_t = jnp.concatenate([o[:, :Q_BLOCK], o[:, Q_BLOCK:]], axis=0)
        o_ref[0, :, g * LANES:(g + 1) * LANES] = o_t.T.astype(o_ref.dtype)


def _moba_attention(qk, vt, kmean, bias, *, width, groups):
    B, S, _ = qk.shape
    nb = S // KEY_BLOCK
    gw = groups * LANES
    chunks = width // gw
    body = functools.partial(_moba_body, nb=nb, groups=groups)
    return pl.pallas_call(
        body,
        grid=(B, chunks, S // Q_BLOCK),
        in_specs=[
            pl.BlockSpec((1, Q_BLOCK, gw), lambda b, p, i: (b, i, p)),
            pl.BlockSpec((1, S, gw), lambda b, p, i: (b, 0, chunks + p)),
            pl.BlockSpec((1, nb, gw, KEY_BLOCK), lambda b, p, i: (b, 0, p, 0)),
            pl.BlockSpec((1, nb, gw), lambda b, p, i: (b, 0, p)),
            _const_spec(bias.shape),
        ],
        out_specs=pl.BlockSpec((1, Q_BLOCK, gw), lambda b, p, i: (b, i, p)),
        out_shape=jax.ShapeDtypeStruct((B, S, width), _BF16),
        scratch_shapes=_attn_scratch(nb, groups, HEAD_DIM + BF16_ROWS),
        compiler_params=pltpu.CompilerParams(
            dimension_semantics=("parallel", "parallel", "arbitrary"),
            vmem_limit_bytes=VMEM_LIMIT_BYTES),
        name="moba_attention",
    )(qk, qk, vt, kmean, bias)


def _diff_body(q_ref, k_ref, vt_ref, bias_ref, lq1_ref, lk1_ref, lq2_ref, lk2_ref, subln_ref, o_ref,
               s0_ref, s1_ref, p_ref, a_ref, m_ref, acc_ref, qt_ref, *, lambda_init, groups):
    step = pl.program_id(2)
    qcats = [_stacked_queries(q_ref, g) for g in range(groups)]
    own_blk = 2 * step + _query_sub_block(groups * GROUP_COLS)

    def active_fn(j):
        return j <= own_blk

    _flash_blocks(step, qcats, k_ref, vt_ref, bias_ref, (s0_ref, s1_ref), p_ref, a_ref, m_ref,
                  acc_ref, qt_ref, active_fn, all_past_active=True, split_heads=False)

    lam = (jnp.exp(jnp.sum(lq1_ref[...] * lk1_ref[...], axis=-1, keepdims=True))
           - jnp.exp(jnp.sum(lq2_ref[...] * lk2_ref[...], axis=-1, keepdims=True))
           + lambda_init)
    for g in range(groups):
        acc = acc_ref[g]
        o = acc[:LANES] / acc[LANES:LANES + 1]
        a_t = o[:, :Q_BLOCK] - lam * o[:, Q_BLOCK:]
        ms = jnp.mean(a_t * a_t, axis=0, keepdims=True)
        a = (a_t * lax.rsqrt(ms + RMS_EPS)).T
        out = (a * subln_ref[...]) * (1.0 - lambda_init)
        o_ref[0, :, g * LANES:(g + 1) * LANES] = out.astype(o_ref.dtype)


def _diff_attention(qk, vt, bias, lq1, lk1, lq2, lk2, subln, *, width, lambda_init, groups):
    B, S, _ = qk.shape
    nb = S // KEY_BLOCK
    gw = groups * LANES
    chunks = width // gw
    body = functools.partial(_diff_body, lambda_init=lambda_init, groups=groups)
    vec = lambda n: _const_spec((1, n))
    return pl.pallas_call(
        body,
        grid=(B, chunks, S // Q_BLOCK),
        in_specs=[
            pl.BlockSpec((1, Q_BLOCK, gw), lambda b, h, i: (b, i, 2 * chunks + h)),
            pl.BlockSpec((1, S, gw), lambda b, h, i: (b, 0, 3 * chunks + h)),
            pl.BlockSpec((1, nb, gw, KEY_BLOCK), lambda b, h, i: (b, 0, chunks + h, 0)),
            _const_spec(bias.shape),
            vec(HEAD_DIM), vec(HEAD_DIM), vec(HEAD_DIM), vec(HEAD_DIM), vec(LANES),
        ],
        out_specs=pl.BlockSpec((1, Q_BLOCK, gw), lambda b, h, i: (b, i, h)),
        out_shape=jax.ShapeDtypeStruct((B, S, width), _BF16),
        scratch_shapes=_attn_scratch(0, groups, LANES + BF16_ROWS),
        compiler_params=pltpu.CompilerParams(
            dimension_semantics=("parallel", "parallel", "arbitrary"),
            vmem_limit_bytes=VMEM_LIMIT_BYTES),
        name="diff_attention",
    )(qk, qk, vt, bias, lq1, lk1, lq2, lk2, subln)


def _out_ffn_body(x_ref, mo_ref, do_ref, wo_ref, gain_ref, wg_ref, wu_ref, wd_ref, y_ref, *, width):
    x = x_ref[...]
    att = (jnp.dot(mo_ref[...], wo_ref[0:width, :], preferred_element_type=_F32)
           + jnp.dot(do_ref[...], wo_ref[width:2 * width, :], preferred_element_type=_F32))
    x1 = x + att
    ms = jnp.mean(x1 * x1, axis=-1, keepdims=True)
    h = (x1 * lax.rsqrt(ms + RMS_EPS) * gain_ref[...]).astype(_BF16)
    g = jnp.dot(h, wg_ref[...], preferred_element_type=_F32)
    u = jnp.dot(h, wu_ref[...], preferred_element_type=_F32)
    a = (g * jax.nn.sigmoid(g) * u).astype(_BF16)
    y_ref[...] = x1 + jnp.dot(a, wd_ref[...], preferred_element_type=_F32)


def _out_ffn(x2, mo2, do2, wo, gain, wg, wu, wd, *, tm):
    T, D = x2.shape
    width = mo2.shape[1]
    F = wg.shape[1]
    body = functools.partial(_out_ffn_body, width=width)
    row = lambda n: pl.BlockSpec((tm, n), lambda t: (t, 0))
    return pl.pallas_call(
        body,
        grid=(T // tm,),
        in_specs=[row(D), row(width), row(width), _const_spec((2 * width, D)), _const_spec((1, D)),
                  _const_spec((D, F)), _const_spec((D, F)), _const_spec((F, D))],
        out_specs=row(D),
        out_shape=jax.ShapeDtypeStruct((T, D), _F32),
        compiler_params=pltpu.CompilerParams(
            dimension_semantics=("parallel",), vmem_limit_bytes=VMEM_LIMIT_BYTES),
        name="outproj_swiglu",
    )(x2, mo2, do2, wo, gain, wg, wu, wd)


def _qk_lane_order(a):
    half = HEAD_DIM // 2
    lead = a.shape[:-1]
    a = a.reshape(*lead, a.shape[-1] // LANES, 2, 2, half)
    return jnp.swapaxes(a, -3, -2).reshape(*lead, -1)


def _rope_tables(seq):
    inv = 1.0 / (ROPE_THETA ** (jnp.arange(0, HEAD_DIM, 2, dtype=_F32) / HEAD_DIM))
    ang = jnp.arange(seq, dtype=_F32)[:, None] * inv[None, :]
    cos = jnp.tile(jnp.cos(ang), (1, 2 * LANES // HEAD_DIM))
    sin = jnp.sin(ang)
    sin = jnp.concatenate([-sin, -sin, sin, sin], axis=-1)
    return cos, sin


def kernel(x, attn_norm, w_in, moba_q_norm, moba_k_norm, diff_q_norm, diff_k_norm,
           lambda_q1, lambda_k1, lambda_q2, lambda_k2, diff_subln, w_out,
           ffn_norm, w_gate, w_up, w_down):
    B, S, D = x.shape
    depth = w_in.shape[0]
    width = w_in.shape[2] // 6
    heads_per_group = width // HEAD_DIM
    assert S % Q_BLOCK == 0 and S // KEY_BLOCK >= Q_SUB, "sequence must be a multiple of 512"
    assert width % (ATTN_GROUPS * LANES) == 0 and w_out.shape[1] == 2 * width
    tm_in = Q_BLOCK
    tm_ffn = Q_BLOCK
    cos, sin = _rope_tables(S)
    bias = _causal_bias()
    gmat = jnp.kron(jnp.eye(MXU_COLS // HEAD_DIM, dtype=_F32),
                    jnp.full((HEAD_DIM, HEAD_DIM), 1.0 / HEAD_DIM, _F32))
    gmat = _qk_lane_order(_qk_lane_order(gmat).T).astype(_BF16)
    scale = HEAD_DIM ** -0.5 * math.log2(math.e)
    for l in range(depth):
        lambda_init = 0.8 - 0.6 * math.exp(-0.3 * l)
        tile = lambda g: _qk_lane_order(jnp.tile(g.astype(_F32), heads_per_group))
        hgain = jnp.stack([tile(moba_q_norm[l]) * scale, tile(moba_k_norm[l]),
                           tile(diff_q_norm[l]) * scale, tile(diff_k_norm[l])])
        w_groups = [w_in[l][:, i * width:(i + 1) * width] for i in range(6)]
        w_groups = [(w if i in (2, 5) else _qk_lane_order(w)).astype(_BF16)
                    for i, w in enumerate(w_groups)]
        qk, vt, kmean = _inproj(x, attn_norm[l][None, :], w_groups, gmat, hgain, cos, sin,
                                tm=tm_in)
        kmean = kmean.reshape(B, S // MOBA_BLOCK, width)
        moba_out = _moba_attention(qk, vt, kmean, bias, width=width, groups=ATTN_GROUPS)
        diff_out = _diff_attention(qk, vt, bias, lambda_q1[l][None, :], lambda_k1[l][None, :],
                                   lambda_q2[l][None, :], lambda_k2[l][None, :],
                                   diff_subln[l][None, :], width=width, lambda_init=lambda_init,
                                   groups=ATTN_GROUPS)
        y = _out_ffn(x.reshape(B * S, D), moba_out.reshape(B * S, width),
                     diff_out.reshape(B * S, width), w_out[l].astype(_BF16), ffn_norm[l][None, :],
                     w_gate[l].astype(_BF16), w_up[l].astype(_BF16), w_down[l].astype(_BF16),
                     tm=tm_ffn)
        x = y.reshape(B, S, D)
    return x
```
